```python
import math
import jax, jax.numpy as jnp
from jax import lax
import numpy as np

D_MODEL = 1024
BATCH = 32
SEQ = 2048
DEPTH = 1
DEC_BATCH = 128
DEC_SEQ = 8
PAST_LEN = 8192
PAGE_SIZE = 128

N_HEADS = 4
HEAD_DIM = 64
V_DIM = 2 * HEAD_DIM
QK_WIDTH = N_HEADS * 2 * HEAD_DIM
ATTN_WIDTH = N_HEADS * V_DIM
POOL_WIDTH = D_MODEL - ATTN_WIDTH
POOL_WINDOWS = (2, 4, 8, 16)
POOL_GROUPS = len(POOL_WINDOWS)
POOL_GC = POOL_WIDTH // POOL_GROUPS
POOL_BUF = max(POOL_WINDOWS) - 1
IN_WIDTH = 2 * QK_WIDTH + ATTN_WIDTH + POOL_WIDTH
D_FF = 4 * D_MODEL
ROPE_THETA = 10000.0
EPS = 1e-6
Q_BLOCK = 128
NEG = -1e30

kernel_name = 'hybrid_diffattn_pool_decode_step'


def rmsnorm(x, g):
    xf = x.astype(jnp.float32)
    r = lax.rsqrt(jnp.mean(xf * xf, axis=-1, keepdims=True) + EPS)
    return (xf * r * g.astype(jnp.float32)).astype(x.dtype)


def lambda_init(layer):
    return 0.8 - 0.6 * math.exp(-0.3 * layer)


def rope(x, pos):
    half = HEAD_DIM // 2
    inv = ROPE_THETA ** (-jnp.arange(0, HEAD_DIM, 2, dtype=jnp.float32) / HEAD_DIM)
    ang = pos.astype(jnp.float32)[:, None] * inv[None, :]
    cos = jnp.cos(ang)[None, :, None, None, :]
    sin = jnp.sin(ang)[None, :, None, None, :]
    xf = x.astype(jnp.float32)
    x1, x2 = xf[..., :half], xf[..., half:]
    return jnp.concatenate([x1 * cos - x2 * sin, x1 * sin + x2 * cos], axis=-1).astype(x.dtype)


def project(x, pos, ln1_g, w_in, q_norm_g, k_norm_g):
    B, T, _ = x.shape
    z = rmsnorm(x, ln1_g) @ w_in
    q, k, v, p = jnp.split(z, [QK_WIDTH, 2 * QK_WIDTH, 2 * QK_WIDTH + ATTN_WIDTH], axis=-1)
    q = rope(rmsnorm(q.reshape(B, T, N_HEADS, 2, HEAD_DIM), q_norm_g), pos)
    k = rope(rmsnorm(k.reshape(B, T, N_HEADS, 2, HEAD_DIM), k_norm_g), pos)
    return q, k, v.reshape(B, T, N_HEADS, V_DIM), p


def diff_lambda(lq1, lk1, lq2, lk2, layer):
    f32 = jnp.float32
    return (jnp.exp(jnp.sum(lq1.astype(f32) * lk1.astype(f32)))
            - jnp.exp(jnp.sum(lq2.astype(f32) * lk2.astype(f32))) + lambda_init(layer))


def prompt_attn(q, k, v, lam):
    B, S = q.shape[:2]
    nb = S // Q_BLOCK
    qb = (q.astype(jnp.float32) * HEAD_DIM ** -0.5).reshape(
        B, nb, Q_BLOCK, N_HEADS, 2, HEAD_DIM).swapaxes(0, 1)
    kf = k.astype(jnp.float32)
    vf = v.astype(jnp.float32)
    kpos = jnp.arange(S)

    def block(args):
        qi, i = args
        s = jnp.einsum('bqhid,bkhid->bhiqk', qi, kf)
        qpos = i * Q_BLOCK + jnp.arange(Q_BLOCK)
        s = jnp.where(kpos[None, :] <= qpos[:, None], s, -jnp.inf)
        pr = jax.nn.softmax(s, axis=-1)
        a = pr[:, :, 0] - lam * pr[:, :, 1]
        return jnp.einsum('bhqk,bkhe->bqhe', a, vf)

    o = lax.map(block, (qb, jnp.arange(nb)))
    return o.swapaxes(0, 1).reshape(B, S, N_HEADS, V_DIM)


def sample_attn(q, k_new, v_new, cache_k_l, cache_v_l, page_table, lam):
    DB, T = q.shape[:2]
    f32 = jnp.float32
    q5 = q.astype(f32) * HEAD_DIM ** -0.5

    def update(carry, s, vp):
        m, l, acc = carry
        m_new = jnp.maximum(m, jnp.max(s, axis=-1))
        corr = jnp.exp(m - m_new)
        p = jnp.exp(s - m_new[..., None])
        l = l * corr + jnp.sum(p, axis=-1)
        acc = acc * corr[..., None] + jnp.einsum('bhitk,bkhe->bhite', p, vp)
        return (m_new, l, acc)

    def step(carry, phys):
        kp = cache_k_l[phys].reshape(DB, PAGE_SIZE, N_HEADS, 2, HEAD_DIM).astype(f32)
        vp = cache_v_l[phys].astype(f32)
        s = jnp.einsum('bthid,bkhid->bhitk', q5, kp)
        return update(carry, s, vp), None

    init = (jnp.full((DB, N_HEADS, 2, T), NEG, f32),
            jnp.zeros((DB, N_HEADS, 2, T), f32),
            jnp.zeros((DB, N_HEADS, 2, T, V_DIM), f32))
    carry, _ = lax.scan(step, init, page_table.T)
    s = jnp.einsum('bthid,bkhid->bhitk', q5, k_new.astype(f32))
    causal = jnp.arange(T)[None, :] <= jnp.arange(T)[:, None]
    s = jnp.where(causal[None, None, None], s, NEG)
    m, l, acc = update(carry, s, v_new.astype(f32))
    o = acc / l[..., None]
    out = o[:, :, 0] - lam * o[:, :, 1]
    return out.transpose(0, 2, 1, 3)


def pool_mix(buf, p, pos0, pool_map, pool_scale):
    B, T, _ = p.shape
    f32 = jnp.float32
    seq = jnp.concatenate([buf.astype(p.dtype), p], axis=1)
    cs = jnp.cumsum(seq.astype(f32), axis=1)
    cs = jnp.concatenate([jnp.zeros((B, 1, POOL_WIDTH), f32), cs], axis=1)
    end = cs[:, POOL_BUF + 1:]
    pos = pos0 + jnp.arange(T)
    pf = p.astype(f32)
    diffs = []
    for g, w in enumerate(POOL_WINDOWS):
        c0, c1 = g * POOL_GC, (g + 1) * POOL_GC
        start = cs[:, POOL_BUF + 1 - w:POOL_BUF + 1 - w + T, c0:c1]
        cnt = jnp.minimum(pos + 1, w).astype(f32)[None, :, None]
        diffs.append((end[..., c0:c1] - start) / cnt - pf[..., c0:c1])
    d = jnp.stack(diffs, axis=2)
    y = jnp.einsum('btgc,gce->btge', d, pool_map.astype(f32)).reshape(B, T, POOL_WIDTH)
    y = y * pool_scale.astype(f32)
    return y.astype(p.dtype), seq[:, -POOL_BUF:]


def finish(x, attn, pool_y, layer, subln_g, w_out, ln2_g, w_up, w_down):
    B, T, _ = x.shape
    a = rmsnorm(attn, subln_g) * (1.0 - lambda_init(layer))
    mix = jnp.concatenate([a.reshape(B, T, ATTN_WIDTH).astype(x.dtype), pool_y.astype(x.dtype)], axis=-1)
    x = x + mix @ w_out
    h = rmsnorm(x, ln2_g)
    return x + jnp.square(jax.nn.relu(h @ w_up)) @ w_down


def setup_inputs(seed: int = 0) -> dict:
    key = jax.random.key(seed)
    ks = jax.random.split(key, 24)
    f32 = jnp.float32
    n_pages = PAST_LEN // PAGE_SIZE
    n_used = DEC_BATCH * n_pages
    n_pool = n_used + max(1, n_used // 4)
    nrm = lambda k, s: jax.random.normal(k, s, f32)
    page_table = jax.random.permutation(ks[5], n_pool)[:n_used].reshape(DEC_BATCH, n_pages).astype(jnp.int32)
    return {
        'x_prompt': nrm(ks[0], (BATCH, SEQ, D_MODEL)),
        'x_sample': nrm(ks[1], (DEC_BATCH, DEC_SEQ, D_MODEL)),
        'cache_k': nrm(ks[2], (DEPTH, n_pool, PAGE_SIZE, N_HEADS, V_DIM)),
        'cache_v': nrm(ks[3], (DEPTH, n_pool, PAGE_SIZE, N_HEADS, V_DIM)),
        'state_pool': nrm(ks[4], (DEPTH, DEC_BATCH, POOL_BUF, POOL_WIDTH)),
        'page_table': page_table,
        'ln1_g': 1.0 + 0.05 * nrm(ks[6], (DEPTH, D_MODEL)),
        'w_in': nrm(ks[7], (DEPTH, D_MODEL, IN_WIDTH)) * D_MODEL ** -0.5,
        'q_norm_g': 1.0 + 0.05 * nrm(ks[8], (DEPTH, HEAD_DIM)),
        'k_norm_g': 1.0 + 0.05 * nrm(ks[9], (DEPTH, HEAD_DIM)),
        'lambda_q1': 0.1 * nrm(ks[10], (DEPTH, HEAD_DIM)),
        'lambda_k1': 0.1 * nrm(ks[11], (DEPTH, HEAD_DIM)),
        'lambda_q2': 0.1 * nrm(ks[12], (DEPTH, HEAD_DIM)),
        'lambda_k2': 0.1 * nrm(ks[13], (DEPTH, HEAD_DIM)),
        'subln_g': 1.0 + 0.05 * nrm(ks[14], (DEPTH, V_DIM)),
        'pool_map': nrm(ks[15], (DEPTH, POOL_GROUPS, POOL_GC, POOL_GC)) * POOL_GC ** -0.5,
        'pool_scale': 1.0 + 0.1 * nrm(ks[16], (DEPTH, POOL_WIDTH)),
        'w_out': nrm(ks[17], (DEPTH, D_MODEL, D_MODEL)) * D_MODEL ** -0.5,
        'ln2_g': 1.0 + 0.05 * nrm(ks[18], (DEPTH, D_MODEL)),
        'w_up': nrm(ks[19], (DEPTH, D_MODEL, D_FF)) * D_MODEL ** -0.5,
        'w_down': nrm(ks[20], (DEPTH, D_FF, D_MODEL)) * D_FF ** -0.5,
    }


def reference(x_prompt, x_sample, cache_k, cache_v, state_pool, page_table, ln1_g, w_in,
              q_norm_g, k_norm_g, lambda_q1, lambda_k1, lambda_q2, lambda_k2, subln_g,
              pool_map, pool_scale, w_out, ln2_g, w_up, w_down):
    B, S, _ = x_prompt.shape
    DB, T, _ = x_sample.shape
    pos_p = jnp.arange(S)
    pos_s = PAST_LEN + jnp.arange(T)
    xp, xs = x_prompt, x_sample
    kp_l, vp_l, pp_l, ks_l, vs_l, ps_l = [], [], [], [], [], []
    for l in range(DEPTH):
        lam = diff_lambda(lambda_q1[l], lambda_k1[l], lambda_q2[l], lambda_k2[l], l)
        q, k, v, p = project(xp, pos_p, ln1_g[l], w_in[l], q_norm_g[l], k_norm_g[l])
        attn = prompt_attn(q, k, v, lam)
        pool_y, pbuf = pool_mix(jnp.zeros((B, POOL_BUF, POOL_WIDTH), p.dtype), p, 0, pool_map[l], pool_scale[l])
        xp = finish(xp, attn, pool_y, l, subln_g[l], w_out[l], ln2_g[l], w_up[l], w_down[l])
        kp_l.append(k.reshape(B, S, N_HEADS, V_DIM))
        vp_l.append(v)
        pp_l.append(pbuf)
        q, k, v, p = project(xs, pos_s, ln1_g[l], w_in[l], q_norm_g[l], k_norm_g[l])
        attn = sample_attn(q, k, v, cache_k[l], cache_v[l], page_table, lam)
        pool_y, sbuf = pool_mix(state_pool[l], p, PAST_LEN, pool_map[l], pool_scale[l])
        xs = finish(xs, attn, pool_y, l, subln_g[l], w_out[l], ln2_g[l], w_up[l], w_down[l])
        ks_l.append(k.reshape(DB, T, N_HEADS, V_DIM))
        vs_l.append(v)
        ps_l.append(sbuf)
    return (xp, xs, jnp.stack(kp_l), jnp.stack(vp_l), jnp.stack(pp_l),
            jnp.stack(ks_l), jnp.stack(vs_l), jnp.stack(ps_l))
```

```python
import functools
import math

import jax
import jax.numpy as jnp
from jax import lax
from jax.experimental import pallas as pl
from jax.experimental.pallas import tpu as pltpu

F32 = jnp.float32
BF16 = jnp.bfloat16

N_HEADS = 4
HEAD_DIM = 64
V_DIM = 2 * HEAD_DIM
QK_WIDTH = N_HEADS * 2 * HEAD_DIM
ATTN_WIDTH = N_HEADS * V_DIM
POOL_WINDOWS = (2, 4, 8, 16)
POOL_BUF = max(POOL_WINDOWS) - 1
ROPE_THETA = 10000.0
EPS = 1e-6
NEG = -1e30
PAST_LEN = 8192
PAGE_SIZE = 128

LANES = 128
SUBLANES = 8
MXU_DIM = 256
VMEM_LIMIT = 56 * 1024 * 1024

HALO = 2 * SUBLANES
TM_PROJ = 512
TQ_ATTN = 512
TM_FIN = 512
FF_CHUNK = 1024
PAGES_PER_STEP = 16


def _lambda_init(layer):
    return 0.8 - 0.6 * math.exp(-0.3 * layer)


def _rms(x, g):
    ms = jnp.mean(x * x, axis=-1, keepdims=True)
    return x * lax.rsqrt(ms + EPS) * g


def _qk_norm_rope(z, gain, gsum, cos, sin):
    ss = jnp.dot((z * z).astype(BF16), gsum, preferred_element_type=F32)
    zn = z * lax.rsqrt(ss * (1.0 / HEAD_DIM) + EPS) * gain
    lane = lax.broadcasted_iota(jnp.int32, (1, LANES), 1)
    first_half = (lane % HEAD_DIM) < (HEAD_DIM // 2)
    outs = []
    for c in range(z.shape[1] // LANES):
        zc = zn[:, c * LANES:(c + 1) * LANES]
        fwd = pltpu.roll(zc, LANES - HEAD_DIM // 2, 1)
        bwd = pltpu.roll(zc, HEAD_DIM // 2, 1)
        outs.append(zc * cos + jnp.where(first_half, fwd, bwd) * sin)
    return outs


def _pool_diffs(hist_rows, p, inv_cnt):
    diffs = []
    for g, w in enumerate(POOL_WINDOWS):
        acc = p[:, g * LANES:(g + 1) * LANES]
        for j in range(1, w):
            acc = acc + hist_rows(j, g)
        diffs.append(acc * inv_cnt[g] - p[:, g * LANES:(g + 1) * LANES])
    return diffs


def _pool_project(diffs, pm_ref, ps_ref):
    outs = []
    for g in range(len(POOL_WINDOWS)):
        y = jnp.dot(diffs[g].astype(BF16), pm_ref[g], preferred_element_type=F32)
        outs.append(y * ps_ref[:, g * LANES:(g + 1) * LANES])
    return outs


def _inproj_prompt_kernel(x_ref, g1_ref, w_ref, gq_ref, gk_ref, gsum_ref, cos_ref, sin_ref,
                          pm_ref, ps_ref,
                          q_ref, k_ref, v_ref, py_ref, tail_ref, pbuf):
    i = pl.program_id(1)
    tm = x_ref.shape[0]
    xn = _rms(x_ref[...], g1_ref[...]).astype(BF16)
    cos = cos_ref[...]
    sin = sin_ref[...]
    gsum = gsum_ref[...]

    for blk in range(QK_WIDTH // MXU_DIM):
        zq = jnp.dot(xn, w_ref[:, blk * MXU_DIM:(blk + 1) * MXU_DIM], preferred_element_type=F32)
        for c, o in enumerate(_qk_norm_rope(zq, gq_ref[...], gsum, cos, sin)):
            col = blk * MXU_DIM + c * LANES
            q_ref[:, col:col + LANES] = (o * (HEAD_DIM ** -0.5)).astype(q_ref.dtype)
    for blk in range(QK_WIDTH // MXU_DIM):
        c0 = QK_WIDTH + blk * MXU_DIM
        zk = jnp.dot(xn, w_ref[:, c0:c0 + MXU_DIM], preferred_element_type=F32)
        for c, o in enumerate(_qk_norm_rope(zk, gk_ref[...], gsum, cos, sin)):
            col = blk * MXU_DIM + c * LANES
            k_ref[:, col:col + LANES] = o
    v_ref[...] = jnp.dot(xn, w_ref[:, 2 * QK_WIDTH:2 * QK_WIDTH + ATTN_WIDTH],
                         preferred_element_type=F32)
    p = jnp.dot(xn, w_ref[:, 2 * QK_WIDTH + ATTN_WIDTH:], preferred_element_type=F32)

    @pl.when(i == 0)
    def _():
        pbuf[0:HALO, :] = jnp.zeros((HALO, p.shape[1]), F32)

    @pl.when(i > 0)
    def _():
        pbuf[0:HALO, :] = pbuf[tm:tm + HALO, :]

    pbuf[HALO:HALO + tm, :] = p

    pos = i * tm + lax.broadcasted_iota(jnp.int32, (tm, 1), 0)
    inv_cnt = [1.0 / jnp.minimum(pos + 1, w).astype(F32) for w in POOL_WINDOWS]

    def hist_rows(j, g):
        return pbuf[HALO - j:HALO - j + tm, g * LANES:(g + 1) * LANES]

    diffs = _pool_diffs(hist_rows, p, inv_cnt)
    for g, y in enumerate(_pool_project(diffs, pm_ref, ps_ref)):
        py_ref[:, g * LANES:(g + 1) * LANES] = y.astype(py_ref.dtype)

    @pl.when(i == pl.num_programs(1) - 1)
    def _():
        tail_ref[...] = pbuf[HALO + tm - POOL_BUF:HALO + tm, :]


def _inproj_prompt(x, g1, w_in, gq, gk, gsum, cos, sin, pm, ps):
    b, s, d = x.shape
    tm = TM_PROJ
    assert s % tm == 0
    in_w = w_in.shape[1]
    pw = in_w - 2 * QK_WIDTH - ATTN_WIDTH
    const2 = lambda bi, i: (0, 0)
    tile = lambda bi, i: (bi, i, 0)
    return pl.pallas_call(
        _inproj_prompt_kernel,
        grid=(b, s // tm),
        in_specs=[
            pl.BlockSpec((None, tm, d), tile),
            pl.BlockSpec((1, d), const2),
            pl.BlockSpec((d, in_w), const2),
            pl.BlockSpec((1, MXU_DIM), const2),
            pl.BlockSpec((1, MXU_DIM), const2),
            pl.BlockSpec((MXU_DIM, MXU_DIM), const2),
            pl.BlockSpec((tm, LANES), lambda bi, i: (i, 0)),
            pl.BlockSpec((tm, LANES), lambda bi, i: (i, 0)),
            pl.BlockSpec((len(POOL_WINDOWS), LANES, LANES), lambda bi, i: (0, 0, 0)),
            pl.BlockSpec((1, pw), const2),
        ],
        out_specs=[
            pl.BlockSpec((None, tm, QK_WIDTH), tile),
            pl.BlockSpec((None, tm, QK_WIDTH), tile),
            pl.BlockSpec((None, tm, ATTN_WIDTH), tile),
            pl.BlockSpec((None, tm, pw), tile),
            pl.BlockSpec((None, POOL_BUF, pw), lambda bi, i: (bi, 0, 0)),
        ],
        out_shape=[
            jax.ShapeDtypeStruct((b, s, QK_WIDTH), BF16),
            jax.ShapeDtypeStruct((b, s, QK_WIDTH), F32),
            jax.ShapeDtypeStruct((b, s, ATTN_WIDTH), F32),
            jax.ShapeDtypeStruct((b, s, pw), BF16),
            jax.ShapeDtypeStruct((b, POOL_BUF, pw), F32),
        ],
        scratch_shapes=[pltpu.VMEM((HALO + tm, pw), F32)],
        compiler_params=pltpu.CompilerParams(
            dimension_semantics=("arbitrary", "arbitrary"), vmem_limit_bytes=VMEM_LIMIT),
        name="inproj_prompt",
    )(x, g1, w_in, gq, gk, gsum, cos, sin, pm, ps)


def _inproj_sample_kernel(past_len, x_ref, g1_ref, w_ref, gq_ref, gk_ref, gsum_ref, cos_ref,
                          sin_ref, pm_ref, ps_ref, state_ref,
                          q_ref, k_ref, v_ref, py_ref, p_ref, hist):
    t = pl.program_id(0)
    pw = p_ref.shape[1]
    xn = _rms(x_ref[...], g1_ref[...]).astype(BF16)
    cos = cos_ref[pl.ds(t, 1), :]
    sin = sin_ref[pl.ds(t, 1), :]
    gsum = gsum_ref[...]

    for blk in range(QK_WIDTH // MXU_DIM):
        zq = jnp.dot(xn, w_ref[:, blk * MXU_DIM:(blk + 1) * MXU_DIM], preferred_element_type=F32)
        for c, o in enumerate(_qk_norm_rope(zq, gq_ref[...], gsum, cos, sin)):
            col = blk * MXU_DIM + c * LANES
            q_ref[:, col:col + LANES] = (o * (HEAD_DIM ** -0.5)).astype(q_ref.dtype)
    for blk in range(QK_WIDTH // MXU_DIM):
        c0 = QK_WIDTH + blk * MXU_DIM
        zk = jnp.dot(xn, w_ref[:, c0:c0 + MXU_DIM], preferred_element_type=F32)
        for c, o in enumerate(_qk_norm_rope(zk, gk_ref[...], gsum, cos, sin)):
            col = blk * MXU_DIM + c * LANES
            k_ref[:, col:col + LANES] = o
    v_ref[...] = jnp.dot(xn, w_ref[:, 2 * QK_WIDTH:2 * QK_WIDTH + ATTN_WIDTH],
                         preferred_element_type=F32)
    p = jnp.dot(xn, w_ref[:, 2 * QK_WIDTH + ATTN_WIDTH:], preferred_element_type=F32)
    p_ref[...] = p

    db = p.shape[0]

    @pl.when(t == 0)
    def _():
        for s in range(POOL_BUF):
            hist[s * db:(s + 1) * db, :] = state_ref[:, s * pw:(s + 1) * pw]

    hist[pl.ds(pl.multiple_of((POOL_BUF + t) * db, SUBLANES), db), :] = p

    pos1 = jnp.full((1, LANES), past_len + 1, jnp.int32) + t
    inv_cnt = [1.0 / jnp.minimum(pos1, w).astype(F32) for w in POOL_WINDOWS]

    def hist_rows(j, g):
        start = pl.multiple_of((POOL_BUF + t - j) * db, SUBLANES)
        return hist[pl.ds(start, db), g * LANES:(g + 1) * LANES]

    diffs = _pool_diffs(hist_rows, p, inv_cnt)
    for g, y in enumerate(_pool_project(diffs, pm_ref, ps_ref)):
        py_ref[:, g * LANES:(g + 1) * LANES] = y.astype(py_ref.dtype)


def _inproj_sample(x2, g1, w_in, gq, gk, gsum, cos, sin, pm, ps, state2, n_t, past_len):
    db = x2.shape[0]
    d = x2.shape[1] // n_t
    in_w = w_in.shape[1]
    pw = in_w - 2 * QK_WIDTH - ATTN_WIDTH
    const2 = lambda t: (0, 0)
    step = lambda t: (0, t)
    return pl.pallas_call(
        functools.partial(_inproj_sample_kernel, past_len),
        grid=(n_t,),
        in_specs=[
            pl.BlockSpec((db, d), step),
            pl.BlockSpec((1, d), const2),
            pl.BlockSpec((d, in_w), const2),
            pl.BlockSpec((1, MXU_DIM), const2),
            pl.BlockSpec((1, MXU_DIM), const2),
            pl.BlockSpec((MXU_DIM, MXU_DIM), const2),
            pl.BlockSpec((n_t, LANES), const2),
            pl.BlockSpec((n_t, LANES), const2),
            pl.BlockSpec((len(POOL_WINDOWS), LANES, LANES), lambda t: (0, 0, 0)),
            pl.BlockSpec((1, pw), const2),
            pl.BlockSpec((db, POOL_BUF * pw), const2),
        ],
        out_specs=[
            pl.BlockSpec((db, QK_WIDTH), step),
            pl.BlockSpec((db, QK_WIDTH), step),
            pl.BlockSpec((db, ATTN_WIDTH), step),
            pl.BlockSpec((db, pw), step),
            pl.BlockSpec((db, pw), step),
        ],
        out_shape=[
            jax.ShapeDtypeStruct((db, n_t * QK_WIDTH), BF16),
            jax.ShapeDtypeStruct((db, n_t * QK_WIDTH), F32),
            jax.ShapeDtypeStruct((db, n_t * ATTN_WIDTH), F32),
            jax.ShapeDtypeStruct((db, n_t * pw), BF16),
            jax.ShapeDtypeStruct((db, n_t * pw), F32),
        ],
        scratch_shapes=[pltpu.VMEM(((POOL_BUF + n_t) * db, pw), F32)],
        compiler_params=pltpu.CompilerParams(
            dimension_semantics=("arbitrary",), vmem_limit_bytes=VMEM_LIMIT),
        name="inproj_sample",
    )(x2, g1, w_in, gq, gk, gsum, cos, sin, pm, ps, state2)


def _diff_lambda(lq1_ref, lk1_ref, lq2_ref, lk2_ref, lam0):
    a = jnp.sum(lq1_ref[...] * lk1_ref[...], axis=-1, keepdims=True)
    b = jnp.sum(lq2_ref[...] * lk2_ref[...], axis=-1, keepdims=True)
    return jnp.exp(a) - jnp.exp(b) + lam0


def _split_maps(q):
    lane = lax.broadcasted_iota(jnp.int32, (1, LANES), 1)
    zero = jnp.zeros_like(q)
    return jnp.concatenate([jnp.where(lane < HEAD_DIM, q, zero),
                            jnp.where(lane >= HEAD_DIM, q, zero)], axis=0)


def _subln_out(acc, l, lam, sg, scale):
    m = acc.shape[0] // 2
    o = acc[:m] / l[:m] - lam * (acc[m:] / l[m:])
    return _rms(o, sg) * scale


def _attn_prompt_kernel(lam0, out_scale, lq1_ref, lk1_ref, lq2_ref, lk2_ref, sg_ref,
                        q_ref, k_ref, v_ref, o_ref, kb, vb, q2, m_s, l_s, acc_s):
    s = q_ref.shape[0]
    tq = TQ_ATTN
    nq = s // tq
    lam = _diff_lambda(lq1_ref, lk1_ref, lq2_ref, lk2_ref, lam0)
    kb[...] = k_ref[...].astype(BF16)
    vb[...] = v_ref[...].astype(BF16)

    def chunk(kstart, masked):
        sc = lax.dot_general(q2[...], kb[pl.ds(kstart, tq), :], (((1,), (1,)), ((), ())),
                             preferred_element_type=F32)
        if masked:
            row = lax.broadcasted_iota(jnp.int32, (2 * tq, tq), 0) % tq
            col = lax.broadcasted_iota(jnp.int32, (2 * tq, tq), 1)
            sc = jnp.where(col <= row, sc, NEG)
        m_prev = m_s[...]
        m_new = jnp.maximum(m_prev, jnp.max(sc, axis=-1, keepdims=True))
        alpha = jnp.exp(m_prev - m_new)
        p = jnp.exp(sc - m_new[:, 0:1])
        l_s[...] = alpha * l_s[...] + jnp.sum(p, axis=-1, keepdims=True)
        acc_s[...] = alpha * acc_s[...] + jnp.dot(p.astype(BF16), vb[pl.ds(kstart, tq), :],
                                                  preferred_element_type=F32)
        m_s[...] = m_new

    def q_block(qi, carry):
        qstart = pl.multiple_of(qi * tq, tq)
        q2[...] = _split_maps(q_ref[pl.ds(qstart, tq), :])
        m_s[...] = jnp.full(m_s.shape, NEG, F32)
        l_s[...] = jnp.zeros(l_s.shape, F32)
        acc_s[...] = jnp.zeros(acc_s.shape, F32)

        def kv_block(kj, c):
            chunk(pl.multiple_of(kj * tq, tq), False)
            return c

        lax.fori_loop(0, qi, kv_block, 0)
        chunk(qstart, True)
        o = _subln_out(acc_s[...], l_s[...], lam, sg_ref[...], out_scale)
        o_ref[pl.ds(qstart, tq), :] = o.astype(o_ref.dtype)
        return carry

    lax.fori_loop(0, nq, q_block, 0)


def _attn_prompt(q, k, v, lq1, lk1, lq2, lk2, sg, layer):
    b, s, _ = q.shape
    tq = TQ_ATTN
    assert s % tq == 0
    vec = lambda bi, h: (0, 0)
    head = lambda bi, h: (bi, 0, h)
    kern = functools.partial(_attn_prompt_kernel, _lambda_init(layer), 1.0 - _lambda_init(layer))
    return pl.pallas_call(
        kern,
        grid=(b, N_HEADS),
        in_specs=[
            pl.BlockSpec((1, HEAD_DIM), vec), pl.BlockSpec((1, HEAD_DIM), vec),
            pl.BlockSpec((1, HEAD_DIM), vec), pl.BlockSpec((1, HEAD_DIM), vec),
            pl.BlockSpec((1, V_DIM), vec),
            pl.BlockSpec((None, s, V_DIM), head),
            pl.BlockSpec((None, s, V_DIM), head),
            pl.BlockSpec((None, s, V_DIM), head),
        ],
        out_specs=pl.BlockSpec((None, s, V_DIM), head),
        out_shape=jax.ShapeDtypeStruct((b, s, ATTN_WIDTH), BF16),
        scratch_shapes=[
            pltpu.VMEM((s, V_DIM), BF16), pltpu.VMEM((s, V_DIM), BF16),
            pltpu.VMEM((2 * tq, V_DIM), BF16),
            pltpu.VMEM((2 * tq, LANES), F32), pltpu.VMEM((2 * tq, LANES), F32),
            pltpu.VMEM((2 * tq, V_DIM), F32),
        ],
        compiler_params=pltpu.CompilerParams(
            dimension_semantics=("arbitrary", "arbitrary"), vmem_limit_bytes=VMEM_LIMIT),
        name="attn_prompt",
    )(lq1, lk1, lq2, lk2, sg, q, k, v)


def _attn_sample_kernel(lam0, out_scale, n_pages_step, pt_ref, lq1_ref, lk1_ref, lq2_ref, lk2_ref,
                        sg_ref, q_ref, kn_ref, vn_ref, *rest):
    k_pages = rest[:n_pages_step]
    v_pages = rest[n_pages_step:2 * n_pages_step]
    o_ref, m_s, l_s, acc_s, knp, vnp = rest[2 * n_pages_step:]
    c = pl.program_id(1)
    n_t = q_ref.shape[0]

    @pl.when(c == 0)
    def _():
        m_s[...] = jnp.full(m_s.shape, NEG, F32)
        l_s[...] = jnp.zeros(l_s.shape, F32)
        acc_s[...] = jnp.zeros(acc_s.shape, F32)

    def update(h, qh, kk, vv, mask):
        sc = lax.dot_general(qh, kk, (((1,), (1,)), ((), ())),
                             preferred_element_type=F32)
        if mask is not None:
            sc = jnp.where(mask, sc, NEG)
        m_prev = m_s[h]
        m_new = jnp.maximum(m_prev, jnp.max(sc, axis=-1, keepdims=True))
        alpha = jnp.exp(m_prev - m_new)
        p = jnp.exp(sc - m_new[:, 0:1])
        l_s[h] = alpha * l_s[h] + jnp.sum(p, axis=-1, keepdims=True)
        acc_s[h] = alpha * acc_s[h] + jnp.dot(p.astype(BF16), vv, preferred_element_type=F32)
        m_s[h] = m_new

    for h in range(N_HEADS):
        cols = slice(h * V_DIM, (h + 1) * V_DIM)
        qh = _split_maps(q_ref[:, cols])
        update(h, qh,
               jnp.concatenate([kp[:, cols].astype(BF16) for kp in k_pages], axis=0),
               jnp.concatenate([vp[:, cols].astype(BF16) for vp in v_pages], axis=0), None)

    @pl.when(c == pl.num_programs(1) - 1)
    def _():
        lam = _diff_lambda(lq1_ref, lk1_ref, lq2_ref, lk2_ref, lam0)
        knp[...] = jnp.zeros(knp.shape, knp.dtype)
        vnp[...] = jnp.zeros(vnp.shape, vnp.dtype)
        knp[0:n_t, :] = kn_ref[...].astype(BF16)
        vnp[0:n_t, :] = vn_ref[...].astype(BF16)
        row = lax.broadcasted_iota(jnp.int32, (2 * n_t, PAGE_SIZE), 0) % n_t
        col = lax.broadcasted_iota(jnp.int32, (2 * n_t, PAGE_SIZE), 1)
        mask = col <= row
        for h in range(N_HEADS):
            cols = slice(h * V_DIM, (h + 1) * V_DIM)
            qh = _split_maps(q_ref[:, cols])
            update(h, qh, knp[:, cols], vnp[:, cols], mask)
            o = _subln_out(acc_s[h], l_s[h], lam, sg_ref[...], out_scale)
            o_ref[:, cols] = o.astype(o_ref.dtype)


def _attn_sample(q3, k3, v3, cache_k3, cache_v3, page_table, lq1, lk1, lq2, lk2, sg, layer):
    db, n_t, _ = q3.shape
    n_pages = page_table.shape[1]
    pps = PAGES_PER_STEP
    assert n_pages % pps == 0
    width = cache_k3.shape[2]
    vec = lambda bi, c, pt: (0, 0)
    seq = lambda bi, c, pt: (bi, 0, 0)

    def page_spec(j):
        return pl.BlockSpec((None, PAGE_SIZE, width),
                            lambda bi, c, pt: (pt[bi * n_pages + c * pps + j], 0, 0))

    kern = functools.partial(_attn_sample_kernel, _lambda_init(layer), 1.0 - _lambda_init(layer), pps)
    grid_spec = pltpu.PrefetchScalarGridSpec(
        num_scalar_prefetch=1,
        grid=(db, n_pages // pps),
        in_specs=[
            pl.BlockSpec((1, HEAD_DIM), vec), pl.BlockSpec((1, HEAD_DIM), vec),
            pl.BlockSpec((1, HEAD_DIM), vec), pl.BlockSpec((1, HEAD_DIM), vec),
            pl.BlockSpec((1, V_DIM), vec),
            pl.BlockSpec((None, n_t, width), seq),
            pl.BlockSpec((None, n_t, width), seq),
            pl.BlockSpec((None, n_t, width), seq),
        ] + [page_spec(j) for j in range(pps)] + [page_spec(j) for j in range(pps)],
        out_specs=pl.BlockSpec((None, n_t, width), seq),
        scratch_shapes=[
            pltpu.VMEM((N_HEADS, 2 * n_t, LANES), F32), pltpu.VMEM((N_HEADS, 2 * n_t, LANES), F32),
            pltpu.VMEM((N_HEADS, 2 * n_t, V_DIM), F32),
            pltpu.VMEM((PAGE_SIZE, width), BF16), pltpu.VMEM((PAGE_SIZE, width), BF16),
        ],
    )
    return pl.pallas_call(
        kern,
        grid_spec=grid_spec,
        out_shape=jax.ShapeDtypeStruct((db, n_t, width), BF16),
        compiler_params=pltpu.CompilerParams(
            dimension_semantics=("arbitrary", "arbitrary"), vmem_limit_bytes=VMEM_LIMIT),
        name="attn_sample",
    )(page_table.reshape(-1), lq1, lk1, lq2, lk2, sg, q3, k3, v3,
      *([cache_k3] * pps), *([cache_v3] * pps))


def _finish_kernel(x_ref, a_ref, py_ref, woa_ref, wop_ref, g2_ref, wu_ref, wd_ref, y_ref):
    x1 = (x_ref[...]
          + jnp.dot(a_ref[...], woa_ref[...], preferred_element_type=F32)
          + jnp.dot(py_ref[...], wop_ref[...], preferred_element_type=F32))
    h = _rms(x1, g2_ref[...]).astype(BF16)
    y_ref[...] = x1
    for c in range(wu_ref.shape[1] // FF_CHUNK):
        u = jnp.dot(h, wu_ref[:, c * FF_CHUNK:(c + 1) * FF_CHUNK], preferred_element_type=F32)
        u = jnp.square(jnp.maximum(u, 0.0)).astype(BF16)
        y_ref[...] += jnp.dot(u, wd_ref[c * FF_CHUNK:(c + 1) * FF_CHUNK, :],
                              preferred_element_type=F32)


def _finish(x, a, py, wo_a, wo_p, g2, w_up, w_down):
    n, d = x.shape
    tm = min(TM_FIN, n)
    assert n % tm == 0
    d_ff = w_up.shape[1]
    tile = lambda i: (i, 0)
    const = lambda i: (0, 0)
    once = pl.Buffered(1)
    return pl.pallas_call(
        _finish_kernel,
        grid=(n // tm,),
        in_specs=[
            pl.BlockSpec((tm, d), tile),
            pl.BlockSpec((tm, a.shape[1]), tile),
            pl.BlockSpec((tm, py.shape[1]), tile),
            pl.BlockSpec(wo_a.shape, const, pipeline_mode=once),
            pl.BlockSpec(wo_p.shape, const, pipeline_mode=once),
            pl.BlockSpec((1, d), const),
            pl.BlockSpec((d, d_ff), const, pipeline_mode=once),
            pl.BlockSpec((d_ff, d), const, pipeline_mode=once),
        ],
        out_specs=pl.BlockSpec((tm, d), tile),
        out_shape=jax.ShapeDtypeStruct((n, d), F32),
        compiler_params=pltpu.CompilerParams(
            dimension_semantics=("arbitrary",), vmem_limit_bytes=VMEM_LIMIT),
        name="finish",
    )(x, a, py, wo_a, wo_p, g2, w_up, w_down)


def _rope_tables(pos):
    half = HEAD_DIM // 2
    inv = ROPE_THETA ** (-jnp.arange(0, HEAD_DIM, 2, dtype=F32) / HEAD_DIM)
    ang = pos.astype(F32)[:, None] * inv[None, :]
    reps = LANES // half
    sign = jnp.tile(jnp.concatenate([-jnp.ones((half,), F32), jnp.ones((half,), F32)]),
                    LANES // HEAD_DIM)
    return jnp.tile(jnp.cos(ang), (1, reps)), jnp.tile(jnp.sin(ang), (1, reps)) * sign[None, :]


def kernel(x_prompt, x_sample, cache_k, cache_v, state_pool, page_table, ln1_g, w_in, q_norm_g, k_norm_g, lambda_q1, lambda_k1, lambda_q2, lambda_k2, subln_g, pool_map, pool_scale, w_out, ln2_g, w_up, w_down):
    b, s, d = x_prompt.shape
    db, n_t, _ = x_sample.shape
    depth = w_in.shape[0]
    n_pool = cache_k.shape[1]
    pw = pool_scale.shape[1]

    idx = jnp.arange(MXU_DIM) // HEAD_DIM
    gsum = (idx[:, None] == idx[None, :]).astype(BF16)
    cos_p, sin_p = _rope_tables(jnp.arange(s))
    cos_s, sin_s = _rope_tables(PAST_LEN + jnp.arange(n_t))
    row = lambda v: v.reshape(1, -1)

    xp = x_prompt
    xs = x_sample.reshape(db * n_t, d)
    outs = [[] for _ in range(6)]
    for l in range(depth):
        w_in_b = w_in[l].astype(BF16)
        gq = row(jnp.tile(q_norm_g[l], MXU_DIM // HEAD_DIM))
        gk = row(jnp.tile(k_norm_g[l], MXU_DIM // HEAD_DIM))
        pm = pool_map[l].astype(BF16)
        ps = row(pool_scale[l])
        wo = w_out[l].astype(BF16)
        wo_a, wo_p = wo[:ATTN_WIDTH], wo[ATTN_WIDTH:]
        wu = w_up[l].astype(BF16)
        wd = w_down[l].astype(BF16)
        lams = [row(v[l]) for v in (lambda_q1, lambda_k1, lambda_q2, lambda_k2)]
        sg = row(subln_g[l])

        q, k, v, py, tail = _inproj_prompt(xp, row(ln1_g[l]), w_in_b, gq, gk, gsum,
                                           cos_p, sin_p, pm, ps)
        a = _attn_prompt(q, k, v, *lams, sg, l)
        xp = _finish(xp.reshape(b * s, d), a.reshape(b * s, ATTN_WIDTH), py.reshape(b * s, pw),
                     wo_a, wo_p, row(ln2_g[l]), wu, wd).reshape(b, s, d)
        outs[0].append(k.reshape(b, s, N_HEADS, V_DIM))
        outs[1].append(v.reshape(b, s, N_HEADS, V_DIM))
        outs[2].append(tail)

        qs, ks, vs, pys, p_new = _inproj_sample(
            xs.reshape(db, n_t * d), row(ln1_g[l]), w_in_b, gq, gk, gsum, cos_s, sin_s, pm, ps,
            state_pool[l].reshape(db, POOL_BUF * pw), n_t, PAST_LEN)
        a_s = _attn_sample(qs.reshape(db, n_t, QK_WIDTH), ks.reshape(db, n_t, QK_WIDTH),
                           vs.reshape(db, n_t, ATTN_WIDTH),
                           cache_k[l].reshape(n_pool, PAGE_SIZE, QK_WIDTH),
                           cache_v[l].reshape(n_pool, PAGE_SIZE, ATTN_WIDTH),
                           page_table, *lams, sg, l)
        xs = _finish(xs, a_s.reshape(db * n_t, ATTN_WIDTH), pys.reshape(db * n_t, pw),
                     wo_a, wo_p, row(ln2_g[l]), wu, wd)
        outs[3].append(ks.reshape(db, n_t, N_HEADS, V_DIM))
        outs[4].append(vs.reshape(db, n_t, N_HEADS, V_DIM))
        seq_tail = jnp.concatenate([state_pool[l], p_new.reshape(db, n_t, pw)], axis=1)
        outs[5].append(seq_tail[:, -POOL_BUF:])

    return (xp, xs.reshape(db, n_t, d)) + tuple(jnp.stack(o) for o in outs)
```

```python
import functools
import math

import jax
import jax.numpy as jnp
from jax import lax
from jax.experimental import pallas as pl
from jax.experimental.pallas import tpu as pltpu

F32 = jnp.float32
BF16 = jnp.bfloat16

N_HEADS = 4
HEAD_DIM = 64
V_DIM = 2 * HEAD_DIM
QK_WIDTH = N_HEADS * 2 * HEAD_DIM
ATTN_WIDTH = N_HEADS * V_DIM
POOL_WINDOWS = (2, 4, 8, 16)
POOL_BUF = max(POOL_WINDOWS) - 1
ROPE_THETA = 10000.0
EPS = 1e-6
NEG = -1e30
PAST_LEN = 8192
PAGE_SIZE = 128

LANES = 128
SUBLANES = 8
MXU_DIM = 256
VMEM_LIMIT = 56 * 1024 * 1024

HALO = 2 * SUBLANES
TM_PROJ = 512
TQ_ATTN = 512
TM_FIN = 512
FF_CHUNK = 1024
PAGES_PER_STEP = 16


def _lambda_init(layer):
    return 0.8 - 0.6 * math.exp(-0.3 * layer)


def _rms(x, g):
    ms = jnp.mean(x * x, axis=-1, keepdims=True)
    return x * lax.rsqrt(ms + EPS) * g


def _qk_norm_rope(z, gain, gsum, cos, sin):
    ss = jnp.dot((z * z).astype(BF16), gsum, preferred_element_type=F32)
    zn = z * lax.rsqrt(ss * (1.0 / HEAD_DIM) + EPS) * gain
    lane = lax.broadcasted_iota(jnp.int32, (1, LANES), 1)
    first_half = (lane % HEAD_DIM) < (HEAD_DIM // 2)
    outs = []
    for c in range(z.shape[1] // LANES):
        zc = zn[:, c * LANES:(c + 1) * LANES]
        fwd = pltpu.roll(zc, LANES - HEAD_DIM // 2, 1)
        bwd = pltpu.roll(zc, HEAD_DIM // 2, 1)
        outs.append(zc * cos + jnp.where(first_half, fwd, bwd) * sin)
    return outs


def _pool_diffs(hist_rows, p, inv_cnt):
    diffs = []
    for g, w in enumerate(POOL_WINDOWS):
        acc = p[:, g * LANES:(g + 1) * LANES]
        for j in range(1, w):
            acc = acc + hist_rows(j, g)
        diffs.append(acc * inv_cnt[g] - p[:, g * LANES:(g + 1) * LANES])
    return diffs


def _pool_project(diffs, pm_ref, ps_ref):
    outs = []
    for g in range(len(POOL_WINDOWS)):
        y = jnp.dot(diffs[g].astype(BF16), pm_ref[g], preferred_element_type=F32)
        outs.append(y * ps_ref[:, g * LANES:(g + 1) * LANES])
    return outs


def _inproj_prompt_kernel(x_ref, g1_ref, w_ref, gq_ref, gk_ref, gsum_ref, cos_ref, sin_ref,
                          pm_ref, ps_ref,
                          q_ref, k_ref, v_ref, py_ref, tail_ref, pbuf):
    i = pl.program_id(1)
    tm = x_ref.shape[0]
    xn = _rms(x_ref[...], g1_ref[...]).astype(BF16)
    cos = cos_ref[...]
    sin = sin_ref[...]
    gsum = gsum_ref[...]

    for blk in range(QK_WIDTH // MXU_DIM):
        zq = jnp.dot(xn, w_ref[:, blk * MXU_DIM:(blk + 1) * MXU_DIM], preferred_element_type=F32)
        for c, o in enumerate(_qk_norm_rope(zq, gq_ref[...], gsum, cos, sin)):
            col = blk * MXU_DIM + c * LANES
            q_ref[:, col:col + LANES] = (o * (HEAD_DIM ** -0.5)).astype(q_ref.dtype)
    for blk in range(QK_WIDTH // MXU_DIM):
        c0 = QK_WIDTH + blk * MXU_DIM
        zk = jnp.dot(xn, w_ref[:, c0:c0 + MXU_DIM], preferred_element_type=F32)
        for c, o in enumerate(_qk_norm_rope(zk, gk_ref[...], gsum, cos, sin)):
            col = blk * MXU_DIM + c * LANES
            k_ref[:, col:col + LANES] = o
    v_ref[...] = jnp.dot(xn, w_ref[:, 2 * QK_WIDTH:2 * QK_WIDTH + ATTN_WIDTH],
                         preferred_element_type=F32)
    p = jnp.dot(xn, w_ref[:, 2 * QK_WIDTH + ATTN_WIDTH:], preferred_element_type=F32)

    @pl.when(i == 0)
    def _():
        pbuf[0:HALO, :] = jnp.zeros((HALO, p.shape[1]), F32)

    @pl.when(i > 0)
    def _():
        pbuf[0:HALO, :] = pbuf[tm:tm + HALO, :]

    pbuf[HALO:HALO + tm, :] = p

    pos = i * tm + lax.broadcasted_iota(jnp.int32, (tm, 1), 0)
    inv_cnt = [1.0 / jnp.minimum(pos + 1, w).astype(F32) for w in POOL_WINDOWS]

    def hist_rows(j, g):
        return pbuf[HALO - j:HALO - j + tm, g * LANES:(g + 1) * LANES]

    diffs = _pool_diffs(hist_rows, p, inv_cnt)
    for g, y in enumerate(_pool_project(diffs, pm_ref, ps_ref)):
        py_ref[:, g * LANES:(g + 1) * LANES] = y.astype(py_ref.dtype)

    @pl.when(i == pl.num_programs(1) - 1)
    def _():
        tail_ref[...] = pbuf[HALO + tm - POOL_BUF:HALO + tm, :]


def _inproj_prompt(x, g1, w_in, gq, gk, gsum, cos, sin, pm, ps):
    b, s, d = x.shape
    tm = TM_PROJ
    assert s % tm == 0
    in_w = w_in.shape[1]
    pw = in_w - 2 * QK_WIDTH - ATTN_WIDTH
    const2 = lambda bi, i: (0, 0)
    tile = lambda bi, i: (bi, i, 0)
    return pl.pallas_call(
        _inproj_prompt_kernel,
        grid=(b, s // tm),
        in_specs=[
            pl.BlockSpec((None, tm, d), tile),
            pl.BlockSpec((1, d), const2),
            pl.BlockSpec((d, in_w), const2),
            pl.BlockSpec((1, MXU_DIM), const2),
            pl.BlockSpec((1, MXU_DIM), const2),
            pl.BlockSpec((MXU_DIM, MXU_DIM), const2),
            pl.BlockSpec((tm, LANES), lambda bi, i: (i, 0)),
            pl.BlockSpec((tm, LANES), lambda bi, i: (i, 0)),
            pl.BlockSpec((len(POOL_WINDOWS), LANES, LANES), lambda bi, i: (0, 0, 0)),
            pl.BlockSpec((1, pw), const2),
        ],
        out_specs=[
            pl.BlockSpec((None, tm, QK_WIDTH), tile),
            pl.BlockSpec((None, tm, QK_WIDTH), tile),
            pl.BlockSpec((None, tm, ATTN_WIDTH), tile),
            pl.BlockSpec((None, tm, pw), tile),
            pl.BlockSpec((None, POOL_BUF, pw), lambda bi, i: (bi, 0, 0)),
        ],
        out_shape=[
            jax.ShapeDtypeStruct((b, s, QK_WIDTH), BF16),
            jax.ShapeDtypeStruct((b, s, QK_WIDTH), F32),
            jax.ShapeDtypeStruct((b, s, ATTN_WIDTH), F32),
            jax.ShapeDtypeStruct((b, s, pw), BF16),
            jax.ShapeDtypeStruct((b, POOL_BUF, pw), F32),
        ],
        scratch_shapes=[pltpu.VMEM((HALO + tm, pw), F32)],
        compiler_params=pltpu.CompilerParams(
            dimension_semantics=("arbitrary", "arbitrary"), vmem_limit_bytes=VMEM_LIMIT),
        name="inproj_prompt",
    )(x, g1, w_in, gq, gk, gsum, cos, sin, pm, ps)


def _inproj_sample_kernel(past_len, x_ref, g1_ref, w_ref, gq_ref, gk_ref, gsum_ref, cos_ref,
                          sin_ref, pm_ref, ps_ref, state_ref,
                          q_ref, k_ref, v_ref, py_ref, p_ref, hist):
    t = pl.program_id(0)
    pw = p_ref.shape[1]
    xn = _rms(x_ref[...], g1_ref[...]).astype(BF16)
    cos = cos_ref[pl.ds(t, 1), :]
    sin = sin_ref[pl.ds(t, 1), :]
    gsum = gsum_ref[...]

    for blk in range(QK_WIDTH // MXU_DIM):
        zq = jnp.dot(xn, w_ref[:, blk * MXU_DIM:(blk + 1) * MXU_DIM], preferred_element_type=F32)
        for c, o in enumerate(_qk_norm_rope(zq, gq_ref[...], gsum, cos, sin)):
            col = blk * MXU_DIM + c * LANES
            q_ref[:, col:col + LANES] = (o * (HEAD_DIM ** -0.5)).astype(q_ref.dtype)
    for blk in range(QK_WIDTH // MXU_DIM):
        c0 = QK_WIDTH + blk * MXU_DIM
        zk = jnp.dot(xn, w_ref[:, c0:c0 + MXU_DIM], preferred_element_type=F32)
        for c, o in enumerate(_qk_norm_rope(zk, gk_ref[...], gsum, cos, sin)):
            col = blk * MXU_DIM + c * LANES
            k_ref[:, col:col + LANES] = o
    v_ref[...] = jnp.dot(xn, w_ref[:, 2 * QK_WIDTH:2 * QK_WIDTH + ATTN_WIDTH],
                         preferred_element_type=F32)
    p = jnp.dot(xn, w_ref[:, 2 * QK_WIDTH + ATTN_WIDTH:], preferred_element_type=F32)
    p_ref[...] = p

    db = p.shape[0]

    @pl.when(t == 0)
    def _():
        for s in range(POOL_BUF):
            hist[s * db:(s + 1) * db, :] = state_ref[:, s * pw:(s + 1) * pw]

    hist[pl.ds(pl.multiple_of((POOL_BUF + t) * db, SUBLANES), db), :] = p

    pos1 = jnp.full((1, LANES), past_len + 1, jnp.int32) + t
    inv_cnt = [1.0 / jnp.minimum(pos1, w).astype(F32) for w in POOL_WINDOWS]

    def hist_rows(j, g):
        start = pl.multiple_of((POOL_BUF + t - j) * db, SUBLANES)
        return hist[pl.ds(start, db), g * LANES:(g + 1) * LANES]

    diffs = _pool_diffs(hist_rows, p, inv_cnt)
    for g, y in enumerate(_pool_project(diffs, pm_ref, ps_ref)):
        py_ref[:, g * LANES:(g + 1) * LANES] = y.astype(py_ref.dtype)


def _inproj_sample(x2, g1, w_in, gq, gk, gsum, cos, sin, pm, ps, state2, n_t, past_len):
    db = x2.shape[0]
    d = x2.shape[1] // n_t
    in_w = w_in.shape[1]
    pw = in_w - 2 * QK_WIDTH - ATTN_WIDTH
    const2 = lambda t: (0, 0)
    step = lambda t: (0, t)
    return pl.pallas_call(
        functools.partial(_inproj_sample_kernel, past_len),
        grid=(n_t,),
        in_specs=[
            pl.BlockSpec((db, d), step),
            pl.BlockSpec((1, d), const2),
            pl.BlockSpec((d, in_w), const2),
            pl.BlockSpec((1, MXU_DIM), const2),
            pl.BlockSpec((1, MXU_DIM), const2),
            pl.BlockSpec((MXU_DIM, MXU_DIM), const2),
            pl.BlockSpec((n_t, LANES), const2),
            pl.BlockSpec((n_t, LANES), const2),
            pl.BlockSpec((len(POOL_WINDOWS), LANES, LANES), lambda t: (0, 0, 0)),
            pl.BlockSpec((1, pw), const2),
            pl.BlockSpec((db, POOL_BUF * pw), const2),
        ],
        out_specs=[
            pl.BlockSpec((db, QK_WIDTH), step),
            pl.BlockSpec((db, QK_WIDTH), step),
            pl.BlockSpec((db, ATTN_WIDTH), step),
            pl.BlockSpec((db, pw), step),
            pl.BlockSpec((db, pw), step),
        ],
        out_shape=[
            jax.ShapeDtypeStruct((db, n_t * QK_WIDTH), BF16),
            jax.ShapeDtypeStruct((db, n_t * QK_WIDTH), F32),
            jax.ShapeDtypeStruct((db, n_t * ATTN_WIDTH), F32),
            jax.ShapeDtypeStruct((db, n_t * pw), BF16),
            jax.ShapeDtypeStruct((db, n_t * pw), F32),
        ],
        scratch_shapes=[pltpu.VMEM(((POOL_BUF + n_t) * db, pw), F32)],
        compiler_params=pltpu.CompilerParams(
            dimension_semantics=("arbitrary",), vmem_limit_bytes=VMEM_LIMIT),
        name="inproj_sample",
    )(x2, g1, w_in, gq, gk, gsum, cos, sin, pm, ps, state2)


def _diff_lambda(lq1_ref, lk1_ref, lq2_ref, lk2_ref, lam0):
    a = jnp.sum(lq1_ref[...] * lk1_ref[...], axis=-1, keepdims=True)
    b = jnp.sum(lq2_ref[...] * lk2_ref[...], axis=-1, keepdims=True)
    return jnp.exp(a) - jnp.exp(b) + lam0


def _split_maps(q):
    lane = lax.broadcasted_iota(jnp.int32, (1, LANES), 1)
    zero = jnp.zeros_like(q)
    return jnp.concatenate([jnp.where(lane < HEAD_DIM, q, zero),
                            jnp.where(lane >= HEAD_DIM, q, zero)], axis=0)


def _subln_out(acc, l, lam, sg, scale):
    m = acc.shape[0] // 2
    o = acc[:m] / l[:m] - lam * (acc[m:] / l[m:])
    return _rms(o, sg) * scale


def _attn_prompt_kernel(lam0, out_scale, lq1_ref, lk1_ref, lq2_ref, lk2_ref, sg_ref,
                        q_ref, k_ref, v_ref, o_ref, kb, vb, q2, m_s, l_s, acc_s):
    s = q_ref.shape[0]
    tq = TQ_ATTN
    nq = s // tq
    lam = _diff_lambda(lq1_ref, lk1_ref, lq2_ref, lk2_ref, lam0)
    kb[...] = k_ref[...].astype(BF16)
    vb[...] = v_ref[...].astype(BF16)

    def chunk(kstart, masked):
        sc = lax.dot_general(q2[...], kb[pl.ds(kstart, tq), :], (((1,), (1,)), ((), ())),
                             preferred_element_type=F32)
        if masked:
            row = lax.broadcasted_iota(jnp.int32, (2 * tq, tq), 0) % tq
            col = lax.broadcasted_iota(jnp.int32, (2 * tq, tq), 1)
            sc = jnp.where(col <= row, sc, NEG)
        m_prev = m_s[...]
        m_new = jnp.maximum(m_prev, jnp.max(sc, axis=-1, keepdims=True))
        alpha = jnp.exp(m_prev - m_new)
        p = jnp.exp(sc - m_new[:, 0:1])
        l_s[...] = alpha * l_s[...] + jnp.sum(p, axis=-1, keepdims=True)
        acc_s[...] = alpha * acc_s[...] + jnp.dot(p.astype(BF16), vb[pl.ds(kstart, tq), :],
                                                  preferred_element_type=F32)
        m_s[...] = m_new

    def q_block(qi, carry):
        qstart = pl.multiple_of(qi * tq, tq)
        q2[...] = _split_maps(q_ref[pl.ds(qstart, tq), :])
        m_s[...] = jnp.full(m_s.shape, NEG, F32)
        l_s[...] = jnp.zeros(l_s.shape, F32)
        acc_s[...] = jnp.zeros(acc_s.shape, F32)

        def kv_block(kj, c):
            chunk(pl.multiple_of(kj * tq, tq), False)
            return c

        lax.fori_loop(0, qi, kv_block, 0)
        chunk(qstart, True)
        o = _subln_out(acc_s[...], l_s[...], lam, sg_ref[...], out_scale)
        o_ref[pl.ds(qstart, tq), :] = o.astype(o_ref.dtype)
        return carry

    lax.fori_loop(0, nq, q_block, 0)


def _attn_prompt(q, k, v, lq1, lk1, lq2, lk2, sg, layer):
    b, s, _ = q.shape
    tq = TQ_ATTN
    assert s % tq == 0
    vec = lambda bi, h: (0, 0)
    head = lambda bi, h: (bi, 0, h)
    kern = functools.partial(_attn_prompt_kernel, _lambda_init(layer), 1.0 - _lambda_init(layer))
    return pl.pallas_call(
        kern,
        grid=(b, N_HEADS),
        in_specs=[
            pl.BlockSpec((1, HEAD_DIM), vec), pl.BlockSpec((1, HEAD_DIM), vec),
            pl.BlockSpec((1, HEAD_DIM), vec), pl.BlockSpec((1, HEAD_DIM), vec),
            pl.BlockSpec((1, V_DIM), vec),
            pl.BlockSpec((None, s, V_DIM), head),
            pl.BlockSpec((None, s, V_DIM), head),
            pl.BlockSpec((None, s, V_DIM), head),
        ],
        out_specs=pl.BlockSpec((None, s, V_DIM), head),
        out_shape=jax.ShapeDtypeStruct((b, s, ATTN_WIDTH), BF16),
        scratch_shapes=[
            pltpu.VMEM((s, V_DIM), BF16), pltpu.VMEM((s, V_DIM), BF16),
            pltpu.VMEM((2 * tq, V_DIM), BF16),
            pltpu.VMEM((2 * tq, LANES), F32), pltpu.VMEM((2 * tq, LANES), F32),
            pltpu.VMEM((2 * tq, V_DIM), F32),
        ],
        compiler_params=pltpu.CompilerParams(
            dimension_semantics=("arbitrary", "arbitrary"), vmem_limit_bytes=VMEM_LIMIT),
        name="attn_prompt",
    )(lq1, lk1, lq2, lk2, sg, q, k, v)


def _attn_sample_kernel(lam0, out_scale, n_pages_step, pt_ref, lq1_ref, lk1_ref, lq2_ref, lk2_ref,
                        sg_ref, q_ref, kn_ref, vn_ref, *rest):
    k_pages = rest[:n_pages_step]
    v_pages = rest[n_pages_step:2 * n_pages_step]
    o_ref, m_s, l_s, acc_s, knp, vnp = rest[2 * n_pages_step:]
    c = pl.program_id(1)
    n_t = q_ref.shape[0]

    @pl.when(c == 0)
    def _():
        m_s[...] = jnp.full(m_s.shape, NEG, F32)
        l_s[...] = jnp.zeros(l_s.shape, F32)
        acc_s[...] = jnp.zeros(acc_s.shape, F32)

    def update(h, qh, kk, vv, mask):
        sc = lax.dot_general(qh, kk, (((1,), (1,)), ((), ())),
                             preferred_element_type=F32)
        if mask is not None:
            sc = jnp.where(mask, sc, NEG)
        m_prev = m_s[h]
        m_new = jnp.maximum(m_prev, jnp.max(sc, axis=-1, keepdims=True))
        alpha = jnp.exp(m_prev - m_new)
        p = jnp.exp(sc - m_new[:, 0:1])
        l_s[h] = alpha * l_s[h] + jnp.sum(p, axis=-1, keepdims=True)
        acc_s[h] = alpha * acc_s[h] + jnp.dot(p.astype(BF16), vv, preferred_element_type=F32)
        m_s[h] = m_new

    for h in range(N_HEADS):
        cols = slice(h * V_DIM, (h + 1) * V_DIM)
        qh = _split_maps(q_ref[:, cols])
        update(h, qh,
               jnp.concatenate([kp[:, cols].astype(BF16) for kp in k_pages], axis=0),
               jnp.concatenate([vp[:, cols].astype(BF16) for vp in v_pages], axis=0), None)

    @pl.when(c == pl.num_programs(1) - 1)
    def _():
        lam = _diff_lambda(lq1_ref, lk1_ref, lq2_ref, lk2_ref, lam0)
        knp[...] = jnp.zeros(knp.shape, knp.dtype)
        vnp[...] = jnp.zeros(vnp.shape, vnp.dtype)
        knp[0:n_t, :] = kn_ref[...].astype(BF16)
        vnp[0:n_t, :] = vn_ref[...].astype(BF16)
        row = lax.broadcasted_iota(jnp.int32, (2 * n_t, PAGE_SIZE), 0) % n_t
        col = lax.broadcasted_iota(jnp.int32, (2 * n_t, PAGE_SIZE), 1)
        mask = col <= row
        for h in range(N_HEADS):
            cols = slice(h * V_DIM, (h + 1) * V_DIM)
            qh = _split_maps(q_ref[:, cols])
            update(h, qh, knp[:, cols], vnp[:, cols], mask)
            o = _subln_out(acc_s[h], l_s[h], lam, sg_ref[...], out_scale)
            o_ref[:, cols] = o.astype(o_ref.dtype)


def _attn_sample(q3, k3, v3, cache_k3, cache_v3, page_table, lq1, lk1, lq2, lk2, sg, layer,
                 page_base):
    db, n_t, _ = q3.shape
    n_pages = page_table.shape[1]
    pps = PAGES_PER_STEP
    assert n_pages % pps == 0
    width = cache_k3.shape[2]
    vec = lambda bi, c, pt: (0, 0)
    seq = lambda bi, c, pt: (bi, 0, 0)

    def page_spec(j):
        return pl.BlockSpec(
            (None, PAGE_SIZE, width),
            lambda bi, c, pt: (page_base + pt[bi * n_pages + c * pps + j], 0, 0))

    kern = functools.partial(_attn_sample_kernel, _lambda_init(layer), 1.0 - _lambda_init(layer), pps)
    grid_spec = pltpu.PrefetchScalarGridSpec(
        num_scalar_prefetch=1,
        grid=(db, n_pages // pps),
        in_specs=[
            pl.BlockSpec((1, HEAD_DIM), vec), pl.BlockSpec((1, HEAD_DIM), vec),
            pl.BlockSpec((1, HEAD_DIM), vec), pl.BlockSpec((1, HEAD_DIM), vec),
            pl.BlockSpec((1, V_DIM), vec),
            pl.BlockSpec((None, n_t, width), seq),
            pl.BlockSpec((None, n_t, width), seq),
            pl.BlockSpec((None, n_t, width), seq),
        ] + [page_spec(j) for j in range(pps)] + [page_spec(j) for j in range(pps)],
        out_specs=pl.BlockSpec((None, n_t, width), seq),
        scratch_shapes=[
            pltpu.VMEM((N_HEADS, 2 * n_t, LANES), F32), pltpu.VMEM((N_HEADS, 2 * n_t, LANES), F32),
            pltpu.VMEM((N_HEADS, 2 * n_t, V_DIM), F32),
            pltpu.VMEM((PAGE_SIZE, width), BF16), pltpu.VMEM((PAGE_SIZE, width), BF16),
        ],
    )
    return pl.pallas_call(
        kern,
        grid_spec=grid_spec,
        out_shape=jax.ShapeDtypeStruct((db, n_t, width), BF16),
        compiler_params=pltpu.CompilerParams(
            dimension_semantics=("arbitrary", "arbitrary"), vmem_limit_bytes=VMEM_LIMIT),
        name="attn_sample",
    )(page_table.reshape(-1), lq1, lk1, lq2, lk2, sg, q3, k3, v3,
      *([cache_k3] * pps), *([cache_v3] * pps))


def _finish_kernel(x_ref, a_ref, py_ref, woa_ref, wop_ref, g2_ref, wu_ref, wd_ref, y_ref):
    x1 = (x_ref[...]
          + jnp.dot(a_ref[...], woa_ref[...], preferred_element_type=F32)
          + jnp.dot(py_ref[...], wop_ref[...], preferred_element_type=F32))
    h = _rms(x1, g2_ref[...]).astype(BF16)
    y_ref[...] = x1
    for c in range(wu_ref.shape[1] // FF_CHUNK):
        u = jnp.dot(h, wu_ref[:, c * FF_CHUNK:(c + 1) * FF_CHUNK], preferred_element_type=F32)
        u = jnp.square(jnp.maximum(u, 0.0)).astype(BF16)
        y_ref[...] += jnp.dot(u, wd_ref[c * FF_CHUNK:(c + 1) * FF_CHUNK, :],
                              preferred_element_type=F32)


def _finish(x, a, py, wo_a, wo_p, g2, w_up, w_down):
    n, d = x.shape
    tm = min(TM_FIN, n)
    assert n % tm == 0
    d_ff = w_up.shape[1]
    tile = lambda i: (i, 0)
    const = lambda i: (0, 0)
    once = pl.Buffered(1)
    return pl.pallas_call(
        _finish_kernel,
        grid=(n // tm,),
        in_specs=[
            pl.BlockSpec((tm, d), tile),
            pl.BlockSpec((tm, a.shape[1]), tile),
            pl.BlockSpec((tm, py.shape[1]), tile),
            pl.BlockSpec(wo_a.shape, const, pipeline_mode=once),
            pl.BlockSpec(wo_p.shape, const, pipeline_mode=once),
            pl.BlockSpec((1, d), const),
            pl.BlockSpec((d, d_ff), const, pipeline_mode=once),
            pl.BlockSpec((d_ff, d), const, pipeline_mode=once),
        ],
        out_specs=pl.BlockSpec((tm, d), tile),
        out_shape=jax.ShapeDtypeStruct((n, d), F32),
        compiler_params=pltpu.CompilerParams(
            dimension_semantics=("arbitrary",), vmem_limit_bytes=VMEM_LIMIT),
        name="finish",
    )(x, a, py, wo_a, wo_p, g2, w_up, w_down)


def _rope_tables(pos):
    half = HEAD_DIM // 2
    inv = ROPE_THETA ** (-jnp.arange(0, HEAD_DIM, 2, dtype=F32) / HEAD_DIM)
    ang = pos.astype(F32)[:, None] * inv[None, :]
    reps = LANES // half
    sign = jnp.tile(jnp.concatenate([-jnp.ones((half,), F32), jnp.ones((half,), F32)]),
                    LANES // HEAD_DIM)
    return jnp.tile(jnp.cos(ang), (1, reps)), jnp.tile(jnp.sin(ang), (1, reps)) * sign[None, :]


def kernel(x_prompt, x_sample, cache_k, cache_v, state_pool, page_table, ln1_g, w_in, q_norm_g, k_norm_g, lambda_q1, lambda_k1, lambda_q2, lambda_k2, subln_g, pool_map, pool_scale, w_out, ln2_g, w_up, w_down):
    b, s, d = x_prompt.shape
    db, n_t, _ = x_sample.shape
    depth = w_in.shape[0]
    n_pool = cache_k.shape[1]
    pw = pool_scale.shape[1]

    idx = jnp.arange(MXU_DIM) // HEAD_DIM
    gsum = (idx[:, None] == idx[None, :]).astype(BF16)
    cos_p, sin_p = _rope_tables(jnp.arange(s))
    cos_s, sin_s = _rope_tables(PAST_LEN + jnp.arange(n_t))
    row = lambda v: v.reshape(1, -1)

    xp = x_prompt
    xs = x_sample.reshape(db * n_t, d)
    outs = [[] for _ in range(6)]
    for l in range(depth):
        w_in_b = w_in[l].astype(BF16)
        gq = row(jnp.tile(q_norm_g[l], MXU_DIM // HEAD_DIM))
        gk = row(jnp.tile(k_norm_g[l], MXU_DIM // HEAD_DIM))
        pm = pool_map[l].astype(BF16)
        ps = row(pool_scale[l])
        wo = w_out[l].astype(BF16)
        wo_a, wo_p = wo[:ATTN_WIDTH], wo[ATTN_WIDTH:]
        wu = w_up[l].astype(BF16)
        wd = w_down[l].astype(BF16)
        lams = [row(v[l]) for v in (lambda_q1, lambda_k1, lambda_q2, lambda_k2)]
        sg = row(subln_g[l])

        q, k, v, py, tail = _inproj_prompt(xp, row(ln1_g[l]), w_in_b, gq, gk, gsum,
                                           cos_p, sin_p, pm, ps)
        a = _attn_prompt(q, k, v, *lams, sg, l)
        xp = _finish(xp.reshape(b * s, d), a.reshape(b * s, ATTN_WIDTH), py.reshape(b * s, pw),
                     wo_a, wo_p, row(ln2_g[l]), wu, wd).reshape(b, s, d)
        outs[0].append(k.reshape(b, s, N_HEADS, V_DIM))
        outs[1].append(v.reshape(b, s, N_HEADS, V_DIM))
        outs[2].append(tail)

        qs, ks, vs, pys, p_new = _inproj_sample(
            xs.reshape(db, n_t * d), row(ln1_g[l]), w_in_b, gq, gk, gsum, cos_s, sin_s, pm, ps,
            state_pool[l].reshape(db, POOL_BUF * pw), n_t, PAST_LEN)
        a_s = _attn_sample(qs.reshape(db, n_t, QK_WIDTH), ks.reshape(db, n_t, QK_WIDTH),
                           vs.reshape(db, n_t, ATTN_WIDTH),
                           cache_k.reshape(depth * n_pool, PAGE_SIZE, QK_WIDTH),
                           cache_v.reshape(depth * n_pool, PAGE_SIZE, ATTN_WIDTH),
                           page_table, *lams, sg, l, l * n_pool)
        xs = _finish(xs, a_s.reshape(db * n_t, ATTN_WIDTH), pys.reshape(db * n_t, pw),
                     wo_a, wo_p, row(ln2_g[l]), wu, wd)
        outs[3].append(ks.reshape(db, n_t, N_HEADS, V_DIM))
        outs[4].append(vs.reshape(db, n_t, N_HEADS, V_DIM))
        seq_tail = jnp.concatenate([state_pool[l], p_new.reshape(db, n_t, pw)], axis=1)
        outs[5].append(seq_tail[:, -POOL_BUF:])

    return (xp, xs.reshape(db, n_t, d)) + tuple(jnp.stack(o) for o in outs)
```

```python
import functools
import math

import jax
import jax.numpy as jnp
from jax import lax
from jax.experimental import pallas as pl
from jax.experimental.pallas import tpu as pltpu

F32 = jnp.float32
BF16 = jnp.bfloat16

N_HEADS = 4
HEAD_DIM = 64
V_DIM = 2 * HEAD_DIM
QK_WIDTH = N_HEADS * 2 * HEAD_DIM
ATTN_WIDTH = N_HEADS * V_DIM
POOL_WINDOWS = (2, 4, 8, 16)
POOL_BUF = max(POOL_WINDOWS) - 1
ROPE_THETA = 10000.0
EPS = 1e-6
NEG = -1e30
Q_SCALE = HEAD_DIM ** -0.5 * math.log2(math.e)
PAST_LEN = 8192
PAGE_SIZE = 128

LANES = 128
SUBLANES = 8
MXU_DIM = 256
VMEM_LIMIT = 56 * 1024 * 1024

HALO = 2 * SUBLANES
TM_PROJ = 512
TQ_ATTN = 512
ROW_BLOCK = 256
TM_FIN = 512
FF_CHUNK = 1024
PAGES_PER_STEP = 16


def _lambda_init(layer):
    return 0.8 - 0.6 * math.exp(-0.3 * layer)


def _rms(x, g):
    ms = jnp.mean(x * x, axis=-1, keepdims=True)
    return x * lax.rsqrt(ms + EPS) * g


def _qk_norm_rope(z, gain, gsum, cos, sin):
    ss = jnp.dot((z * z).astype(BF16), gsum, preferred_element_type=F32)
    zn = z * lax.rsqrt(ss * (1.0 / HEAD_DIM) + EPS) * gain
    lane = lax.broadcasted_iota(jnp.int32, (1, LANES), 1)
    first_half = (lane % HEAD_DIM) < (HEAD_DIM // 2)
    outs = []
    for c in range(z.shape[1] // LANES):
        zc = zn[:, c * LANES:(c + 1) * LANES]
        fwd = pltpu.roll(zc, LANES - HEAD_DIM // 2, 1)
        bwd = pltpu.roll(zc, HEAD_DIM // 2, 1)
        outs.append(zc * cos + jnp.where(first_half, fwd, bwd) * sin)
    return outs


def _pool_diffs(hist_rows, p, inv_cnt):
    diffs = []
    for g, w in enumerate(POOL_WINDOWS):
        acc = p[:, g * LANES:(g + 1) * LANES]
        for j in range(1, w):
            acc = acc + hist_rows(j, g)
        diffs.append(acc * inv_cnt[g] - p[:, g * LANES:(g + 1) * LANES])
    return diffs


def _pool_project(diffs, pm_ref, ps_ref):
    outs = []
    for g in range(len(POOL_WINDOWS)):
        y = jnp.dot(diffs[g].astype(BF16), pm_ref[g], preferred_element_type=F32)
        outs.append(y * ps_ref[:, g * LANES:(g + 1) * LANES])
    return outs


def _inproj_prompt_kernel(x_ref, g1_ref, w_ref, gq_ref, gk_ref, gsum_ref, cos_ref, sin_ref,
                          pm_ref, ps_ref,
                          q_ref, k_ref, v_ref, py_ref, tail_ref, pbuf):
    i = pl.program_id(1)
    tm = x_ref.shape[0]
    xn = _rms(x_ref[...], g1_ref[...]).astype(BF16)
    cos = cos_ref[...]
    sin = sin_ref[...]
    gsum = gsum_ref[...]

    for blk in range(QK_WIDTH // MXU_DIM):
        zq = jnp.dot(xn, w_ref[:, blk * MXU_DIM:(blk + 1) * MXU_DIM], preferred_element_type=F32)
        for c, o in enumerate(_qk_norm_rope(zq, gq_ref[...], gsum, cos, sin)):
            col = blk * MXU_DIM + c * LANES
            q_ref[:, col:col + LANES] = (o * Q_SCALE).astype(q_ref.dtype)
    heads_per_blk = MXU_DIM // V_DIM
    for blk in range(QK_WIDTH // MXU_DIM):
        c0 = QK_WIDTH + blk * MXU_DIM
        zk = jnp.dot(xn, w_ref[:, c0:c0 + MXU_DIM], preferred_element_type=F32)
        for c, o in enumerate(_qk_norm_rope(zk, gk_ref[...], gsum, cos, sin)):
            k_ref[pl.ds(blk * heads_per_blk + c, tm, stride=N_HEADS), :] = o
    v = jnp.dot(xn, w_ref[:, 2 * QK_WIDTH:2 * QK_WIDTH + ATTN_WIDTH], preferred_element_type=F32)
    for h in range(N_HEADS):
        v_ref[pl.ds(h, tm, stride=N_HEADS), :] = v[:, h * V_DIM:(h + 1) * V_DIM]
    p = jnp.dot(xn, w_ref[:, 2 * QK_WIDTH + ATTN_WIDTH:], preferred_element_type=F32)

    @pl.when(i == 0)
    def _():
        pbuf[0:HALO, :] = jnp.zeros((HALO, p.shape[1]), F32)

    @pl.when(i > 0)
    def _():
        pbuf[0:HALO, :] = pbuf[tm:tm + HALO, :]

    pbuf[HALO:HALO + tm, :] = p

    pos = i * tm + lax.broadcasted_iota(jnp.int32, (tm, 1), 0)
    inv_cnt = [1.0 / jnp.minimum(pos + 1, w).astype(F32) for w in POOL_WINDOWS]

    def hist_rows(j, g):
        return pbuf[HALO - j:HALO - j + tm, g * LANES:(g + 1) * LANES]

    diffs = _pool_diffs(hist_rows, p, inv_cnt)
    for g, y in enumerate(_pool_project(diffs, pm_ref, ps_ref)):
        py_ref[:, g * LANES:(g + 1) * LANES] = y.astype(py_ref.dtype)

    @pl.when(i == pl.num_programs(1) - 1)
    def _():
        tail_ref[...] = pbuf[HALO + tm - POOL_BUF:HALO + tm, :]


def _inproj_prompt(x, g1, w_in, gq, gk, gsum, cos, sin, pm, ps):
    b, s, d = x.shape
    tm = TM_PROJ
    assert s % tm == 0
    in_w = w_in.shape[1]
    pw = in_w - 2 * QK_WIDTH - ATTN_WIDTH
    const2 = lambda bi, i: (0, 0)
    tile = lambda bi, i: (bi, i, 0)
    return pl.pallas_call(
        _inproj_prompt_kernel,
        grid=(b, s // tm),
        in_specs=[
            pl.BlockSpec((None, tm, d), tile),
            pl.BlockSpec((1, d), const2),
            pl.BlockSpec((d, in_w), const2),
            pl.BlockSpec((1, MXU_DIM), const2),
            pl.BlockSpec((1, MXU_DIM), const2),
            pl.BlockSpec((MXU_DIM, MXU_DIM), const2),
            pl.BlockSpec((tm, LANES), lambda bi, i: (i, 0)),
            pl.BlockSpec((tm, LANES), lambda bi, i: (i, 0)),
            pl.BlockSpec((len(POOL_WINDOWS), LANES, LANES), lambda bi, i: (0, 0, 0)),
            pl.BlockSpec((1, pw), const2),
        ],
        out_specs=[
            pl.BlockSpec((None, tm, QK_WIDTH), tile),
            pl.BlockSpec((tm * N_HEADS, V_DIM), lambda bi, i: (bi * (s // tm) + i, 0)),
            pl.BlockSpec((tm * N_HEADS, V_DIM), lambda bi, i: (bi * (s // tm) + i, 0)),
            pl.BlockSpec((None, tm, pw), tile),
            pl.BlockSpec((None, POOL_BUF, pw), lambda bi, i: (bi, 0, 0)),
        ],
        out_shape=[
            jax.ShapeDtypeStruct((b, s, QK_WIDTH), BF16),
            jax.ShapeDtypeStruct((b * s * N_HEADS, V_DIM), F32),
            jax.ShapeDtypeStruct((b * s * N_HEADS, V_DIM), F32),
            jax.ShapeDtypeStruct((b, s, pw), BF16),
            jax.ShapeDtypeStruct((b, POOL_BUF, pw), F32),
        ],
        scratch_shapes=[pltpu.VMEM((HALO + tm, pw), F32)],
        compiler_params=pltpu.CompilerParams(
            dimension_semantics=("arbitrary", "arbitrary"), vmem_limit_bytes=VMEM_LIMIT),
        name="inproj_prompt",
    )(x, g1, w_in, gq, gk, gsum, cos, sin, pm, ps)


def _inproj_sample_kernel(past_len, x_ref, g1_ref, w_ref, gq_ref, gk_ref, gsum_ref, cos_ref,
                          sin_ref, pm_ref, ps_ref, state_ref,
                          q_ref, k_ref, v_ref, py_ref, p_ref, hist):
    t = pl.program_id(0)
    pw = p_ref.shape[1]
    xn = _rms(x_ref[...], g1_ref[...]).astype(BF16)
    cos = cos_ref[pl.ds(t, 1), :]
    sin = sin_ref[pl.ds(t, 1), :]
    gsum = gsum_ref[...]

    for blk in range(QK_WIDTH // MXU_DIM):
        zq = jnp.dot(xn, w_ref[:, blk * MXU_DIM:(blk + 1) * MXU_DIM], preferred_element_type=F32)
        for c, o in enumerate(_qk_norm_rope(zq, gq_ref[...], gsum, cos, sin)):
            col = blk * MXU_DIM + c * LANES
            q_ref[:, col:col + LANES] = (o * Q_SCALE).astype(q_ref.dtype)
    for blk in range(QK_WIDTH // MXU_DIM):
        c0 = QK_WIDTH + blk * MXU_DIM
        zk = jnp.dot(xn, w_ref[:, c0:c0 + MXU_DIM], preferred_element_type=F32)
        for c, o in enumerate(_qk_norm_rope(zk, gk_ref[...], gsum, cos, sin)):
            col = blk * MXU_DIM + c * LANES
            k_ref[:, col:col + LANES] = o
    v_ref[...] = jnp.dot(xn, w_ref[:, 2 * QK_WIDTH:2 * QK_WIDTH + ATTN_WIDTH],
                         preferred_element_type=F32)
    p = jnp.dot(xn, w_ref[:, 2 * QK_WIDTH + ATTN_WIDTH:], preferred_element_type=F32)
    p_ref[...] = p

    db = p.shape[0]

    @pl.when(t == 0)
    def _():
        for s in range(POOL_BUF):
            hist[s * db:(s + 1) * db, :] = state_ref[:, s * pw:(s + 1) * pw]

    hist[pl.ds(pl.multiple_of((POOL_BUF + t) * db, SUBLANES), db), :] = p

    pos1 = jnp.full((1, LANES), past_len + 1, jnp.int32) + t
    inv_cnt = [1.0 / jnp.minimum(pos1, w).astype(F32) for w in POOL_WINDOWS]

    def hist_rows(j, g):
        start = pl.multiple_of((POOL_BUF + t - j) * db, SUBLANES)
        return hist[pl.ds(start, db), g * LANES:(g + 1) * LANES]

    diffs = _pool_diffs(hist_rows, p, inv_cnt)
    for g, y in enumerate(_pool_project(diffs, pm_ref, ps_ref)):
        py_ref[:, g * LANES:(g + 1) * LANES] = y.astype(py_ref.dtype)


def _inproj_sample(x2, g1, w_in, gq, gk, gsum, cos, sin, pm, ps, state2, n_t, past_len):
    db = x2.shape[0]
    d = x2.shape[1] // n_t
    in_w = w_in.shape[1]
    pw = in_w - 2 * QK_WIDTH - ATTN_WIDTH
    const2 = lambda t: (0, 0)
    step = lambda t: (0, t)
    return pl.pallas_call(
        functools.partial(_inproj_sample_kernel, past_len),
        grid=(n_t,),
        in_specs=[
            pl.BlockSpec((db, d), step),
            pl.BlockSpec((1, d), const2),
            pl.BlockSpec((d, in_w), const2),
            pl.BlockSpec((1, MXU_DIM), const2),
            pl.BlockSpec((1, MXU_DIM), const2),
            pl.BlockSpec((MXU_DIM, MXU_DIM), const2),
            pl.BlockSpec((n_t, LANES), const2),
            pl.BlockSpec((n_t, LANES), const2),
            pl.BlockSpec((len(POOL_WINDOWS), LANES, LANES), lambda t: (0, 0, 0)),
            pl.BlockSpec((1, pw), const2),
            pl.BlockSpec((db, POOL_BUF * pw), const2),
        ],
        out_specs=[
            pl.BlockSpec((db, QK_WIDTH), step),
            pl.BlockSpec((db, QK_WIDTH), step),
            pl.BlockSpec((db, ATTN_WIDTH), step),
            pl.BlockSpec((db, pw), step),
            pl.BlockSpec((db, pw), step),
        ],
        out_shape=[
            jax.ShapeDtypeStruct((db, n_t * QK_WIDTH), BF16),
            jax.ShapeDtypeStruct((db, n_t * QK_WIDTH), F32),
            jax.ShapeDtypeStruct((db, n_t * ATTN_WIDTH), F32),
            jax.ShapeDtypeStruct((db, n_t * pw), BF16),
            jax.ShapeDtypeStruct((db, n_t * pw), F32),
        ],
        scratch_shapes=[pltpu.VMEM(((POOL_BUF + n_t) * db, pw), F32)],
        compiler_params=pltpu.CompilerParams(
            dimension_semantics=("arbitrary",), vmem_limit_bytes=VMEM_LIMIT),
        name="inproj_sample",
    )(x2, g1, w_in, gq, gk, gsum, cos, sin, pm, ps, state2)


def _diff_lambda(lq1_ref, lk1_ref, lq2_ref, lk2_ref, lam0):
    a = jnp.sum(lq1_ref[...] * lk1_ref[...], axis=-1, keepdims=True)
    b = jnp.sum(lq2_ref[...] * lk2_ref[...], axis=-1, keepdims=True)
    return jnp.exp(a) - jnp.exp(b) + lam0


def _split_maps(q):
    lane = lax.broadcasted_iota(jnp.int32, (1, LANES), 1)
    zero = jnp.zeros_like(q)
    return jnp.concatenate([jnp.where(lane < HEAD_DIM, q, zero),
                            jnp.where(lane >= HEAD_DIM, q, zero)], axis=0)


def _subln_out(acc, l, lam, sg, scale):
    m = acc.shape[0] // 2
    o = acc[:m] / l[:m] - lam * (acc[m:] / l[m:])
    return _rms(o, sg) * scale


def _attn_prompt_kernel(lam0, out_scale, lq1_ref, lk1_ref, lq2_ref, lk2_ref, sg_ref,
                        q_ref, k_ref, v_ref, o_ref, kb, vb1, q2, m_s, acc_s):
    s = q_ref.shape[0]
    tq = TQ_ATTN
    rb = ROW_BLOCK
    nq = s // tq
    lam = _diff_lambda(lq1_ref, lk1_ref, lq2_ref, lk2_ref, lam0)
    for h in range(N_HEADS):
        kb[h] = k_ref[pl.ds(h, s, stride=N_HEADS), :].astype(BF16)
        vb1[h, :, 0:V_DIM] = v_ref[pl.ds(h, s, stride=N_HEADS), :].astype(BF16)
        vb1[h, :, V_DIM:] = jnp.ones((s, V_DIM), BF16)

    def chunk(kstart, diagonal):
        for h in range(N_HEADS):
            for r in range(2 * tq // rb):
                rows = slice(r * rb, (r + 1) * rb)
                p0 = (r * rb) % tq
                nk = p0 + rb if diagonal else tq
                sc = lax.dot_general(q2[h, rows, :], kb[h, pl.ds(kstart, nk), :],
                                     (((1,), (1,)), ((), ())), preferred_element_type=F32)
                if diagonal:
                    row = lax.broadcasted_iota(jnp.int32, (rb, rb), 0)
                    col = lax.broadcasted_iota(jnp.int32, (rb, rb), 1)
                    tri = jnp.where(col <= row, sc[:, p0:], NEG)
                    sc = tri if p0 == 0 else jnp.concatenate([sc[:, :p0], tri], axis=1)
                m_prev = m_s[h, rows, :]
                m_new = jnp.maximum(m_prev, jnp.max(sc, axis=-1, keepdims=True))
                alpha = jnp.exp2(m_prev - m_new)
                p = jnp.exp2(sc - jnp.tile(m_new, (1, nk // LANES))).astype(BF16)
                pv = jnp.dot(p, vb1[h, pl.ds(kstart, nk), :], preferred_element_type=F32)
                acc_s[h, rows, :] = jnp.tile(alpha, (1, 2)) * acc_s[h, rows, :] + pv
                m_s[h, rows, :] = m_new

    def q_block(qi, carry):
        qstart = pl.multiple_of(qi * tq, tq)
        for h in range(N_HEADS):
            q2[h] = _split_maps(q_ref[pl.ds(qstart, tq), h * V_DIM:(h + 1) * V_DIM])
        m_s[...] = jnp.full(m_s.shape, NEG, F32)
        acc_s[...] = jnp.zeros(acc_s.shape, F32)

        def kv_block(kj, c):
            chunk(pl.multiple_of(kj * tq, tq), False)
            return c

        lax.fori_loop(0, qi, kv_block, 0)
        chunk(qstart, True)
        for h in range(N_HEADS):
            acc = acc_s[h]
            o = _subln_out(acc[:, 0:V_DIM], acc[:, V_DIM:], lam, sg_ref[...], out_scale)
            o_ref[pl.ds(qstart, tq), h * V_DIM:(h + 1) * V_DIM] = o.astype(o_ref.dtype)
        return carry

    lax.fori_loop(0, nq, q_block, 0)


def _attn_prompt(q, k2, v2, lq1, lk1, lq2, lk2, sg, layer):
    b, s, _ = q.shape
    tq = TQ_ATTN
    assert s % tq == 0 and tq % ROW_BLOCK == 0
    vec = lambda bi: (0, 0)
    kern = functools.partial(_attn_prompt_kernel, _lambda_init(layer), 1.0 - _lambda_init(layer))
    return pl.pallas_call(
        kern,
        grid=(b,),
        in_specs=[
            pl.BlockSpec((1, HEAD_DIM), vec), pl.BlockSpec((1, HEAD_DIM), vec),
            pl.BlockSpec((1, HEAD_DIM), vec), pl.BlockSpec((1, HEAD_DIM), vec),
            pl.BlockSpec((1, V_DIM), vec),
            pl.BlockSpec((None, s, ATTN_WIDTH), lambda bi: (bi, 0, 0)),
            pl.BlockSpec((s * N_HEADS, V_DIM), lambda bi: (bi, 0)),
            pl.BlockSpec((s * N_HEADS, V_DIM), lambda bi: (bi, 0)),
        ],
        out_specs=pl.BlockSpec((None, s, ATTN_WIDTH), lambda bi: (bi, 0, 0)),
        out_shape=jax.ShapeDtypeStruct((b, s, ATTN_WIDTH), BF16),
        scratch_shapes=[
            pltpu.VMEM((N_HEADS, s, V_DIM), BF16), pltpu.VMEM((N_HEADS, s, 2 * V_DIM), BF16),
            pltpu.VMEM((N_HEADS, 2 * tq, V_DIM), BF16),
            pltpu.VMEM((N_HEADS, 2 * tq, LANES), F32),
            pltpu.VMEM((N_HEADS, 2 * tq, 2 * V_DIM), F32),
        ],
        compiler_params=pltpu.CompilerParams(
            dimension_semantics=("arbitrary",), vmem_limit_bytes=VMEM_LIMIT),
        name="attn_prompt",
    )(lq1, lk1, lq2, lk2, sg, q, k2, v2)


def _attn_sample_kernel(lam0, out_scale, n_pages_step, pt_ref, lq1_ref, lk1_ref, lq2_ref, lk2_ref,
                        sg_ref, q_ref, kn_ref, vn_ref, *rest):
    k_pages = rest[:n_pages_step]
    v_pages = rest[n_pages_step:2 * n_pages_step]
    o_ref, m_s, l_s, acc_s, knp, vnp = rest[2 * n_pages_step:]
    c = pl.program_id(1)
    n_t = q_ref.shape[0]

    @pl.when(c == 0)
    def _():
        m_s[...] = jnp.full(m_s.shape, NEG, F32)
        l_s[...] = jnp.zeros(l_s.shape, F32)
        acc_s[...] = jnp.zeros(acc_s.shape, F32)

    def update(h, qh, kk, vv, mask):
        sc = lax.dot_general(qh, kk, (((1,), (1,)), ((), ())),
                             preferred_element_type=F32)
        if mask is not None:
            sc = jnp.where(mask, sc, NEG)
        m_prev = m_s[h]
        m_new = jnp.maximum(m_prev, jnp.max(sc, axis=-1, keepdims=True))
        alpha = jnp.exp2(m_prev - m_new)
        p = jnp.exp2(sc - m_new[:, 0:1])
        l_s[h] = alpha * l_s[h] + jnp.sum(p, axis=-1, keepdims=True)
        acc_s[h] = alpha * acc_s[h] + jnp.dot(p.astype(BF16), vv, preferred_element_type=F32)
        m_s[h] = m_new

    def head_rows(page_ref, h):
        return page_ref[pl.ds(h, PAGE_SIZE, stride=N_HEADS), :].astype(BF16)

    for h in range(N_HEADS):
        cols = slice(h * V_DIM, (h + 1) * V_DIM)
        qh = _split_maps(q_ref[:, cols])
        update(h, qh,
               jnp.concatenate([head_rows(kp, h) for kp in k_pages], axis=0),
               jnp.concatenate([head_rows(vp, h) for vp in v_pages], axis=0), None)

    @pl.when(c == pl.num_programs(1) - 1)
    def _():
        lam = _diff_lambda(lq1_ref, lk1_ref, lq2_ref, lk2_ref, lam0)
        knp[...] = jnp.zeros(knp.shape, knp.dtype)
        vnp[...] = jnp.zeros(vnp.shape, vnp.dtype)
        knp[0:n_t, :] = kn_ref[...].astype(BF16)
        vnp[0:n_t, :] = vn_ref[...].astype(BF16)
        row = lax.broadcasted_iota(jnp.int32, (2 * n_t, PAGE_SIZE), 0) % n_t
        col = lax.broadcasted_iota(jnp.int32, (2 * n_t, PAGE_SIZE), 1)
        mask = col <= row
        for h in range(N_HEADS):
            cols = slice(h * V_DIM, (h + 1) * V_DIM)
            qh = _split_maps(q_ref[:, cols])
            update(h, qh, knp[:, cols], vnp[:, cols], mask)
            o = _subln_out(acc_s[h], l_s[h], lam, sg_ref[...], out_scale)
            o_ref[:, cols] = o.astype(o_ref.dtype)


def _attn_sample(q3, k3, v3, cache_k2, cache_v2, page_table, lq1, lk1, lq2, lk2, sg, layer,
                 page_base):
    db, n_t, width = q3.shape
    n_pages = page_table.shape[1]
    pps = PAGES_PER_STEP
    assert n_pages % pps == 0
    vec = lambda bi, c, pt: (0, 0)
    seq = lambda bi, c, pt: (bi, 0, 0)

    def page_spec(j):
        return pl.BlockSpec(
            (PAGE_SIZE * N_HEADS, V_DIM),
            lambda bi, c, pt: (page_base + pt[bi * n_pages + c * pps + j], 0))

    kern = functools.partial(_attn_sample_kernel, _lambda_init(layer), 1.0 - _lambda_init(layer), pps)
    grid_spec = pltpu.PrefetchScalarGridSpec(
        num_scalar_prefetch=1,
        grid=(db, n_pages // pps),
        in_specs=[
            pl.BlockSpec((1, HEAD_DIM), vec), pl.BlockSpec((1, HEAD_DIM), vec),
            pl.BlockSpec((1, HEAD_DIM), vec), pl.BlockSpec((1, HEAD_DIM), vec),
            pl.BlockSpec((1, V_DIM), vec),
            pl.BlockSpec((None, n_t, width), seq),
            pl.BlockSpec((None, n_t, width), seq),
            pl.BlockSpec((None, n_t, width), seq),
        ] + [page_spec(j) for j in range(pps)] + [page_spec(j) for j in range(pps)],
        out_specs=pl.BlockSpec((None, n_t, width), seq),
        scratch_shapes=[
            pltpu.VMEM((N_HEADS, 2 * n_t, LANES), F32), pltpu.VMEM((N_HEADS, 2 * n_t, LANES), F32),
            pltpu.VMEM((N_HEADS, 2 * n_t, V_DIM), F32),
            pltpu.VMEM((PAGE_SIZE, width), BF16), pltpu.VMEM((PAGE_SIZE, width), BF16),
        ],
    )
    return pl.pallas_call(
        kern,
        grid_spec=grid_spec,
        out_shape=jax.ShapeDtypeStruct((db, n_t, width), BF16),
        compiler_params=pltpu.CompilerParams(
            dimension_semantics=("arbitrary", "arbitrary"), vmem_limit_bytes=VMEM_LIMIT),
        name="attn_sample",
    )(page_table.reshape(-1), lq1, lk1, lq2, lk2, sg, q3, k3, v3,
      *([cache_k2] * pps), *([cache_v2] * pps))


def _finish_kernel(x_ref, a_ref, py_ref, woa_ref, wop_ref, g2_ref, wu_ref, wd_ref, y_ref):
    x1 = (x_ref[...]
          + jnp.dot(a_ref[...], woa_ref[...], preferred_element_type=F32)
          + jnp.dot(py_ref[...], wop_ref[...], preferred_element_type=F32))
    h = _rms(x1, g2_ref[...]).astype(BF16)
    y_ref[...] = x1
    for c in range(wu_ref.shape[1] // FF_CHUNK):
        u = jnp.dot(h, wu_ref[:, c * FF_CHUNK:(c + 1) * FF_CHUNK], preferred_element_type=F32)
        u = jnp.square(jnp.maximum(u, 0.0)).astype(BF16)
        y_ref[...] += jnp.dot(u, wd_ref[c * FF_CHUNK:(c + 1) * FF_CHUNK, :],
                              preferred_element_type=F32)


def _finish(x, a, py, wo_a, wo_p, g2, w_up, w_down):
    n, d = x.shape
    tm = min(TM_FIN, n)
    assert n % tm == 0
    d_ff = w_up.shape[1]
    tile = lambda i: (i, 0)
    const = lambda i: (0, 0)
    once = pl.Buffered(1)
    return pl.pallas_call(
        _finish_kernel,
        grid=(n // tm,),
        in_specs=[
            pl.BlockSpec((tm, d), tile),
            pl.BlockSpec((tm, a.shape[1]), tile),
            pl.BlockSpec((tm, py.shape[1]), tile),
            pl.BlockSpec(wo_a.shape, const, pipeline_mode=once),
            pl.BlockSpec(wo_p.shape, const, pipeline_mode=once),
            pl.BlockSpec((1, d), const),
            pl.BlockSpec((d, d_ff), const, pipeline_mode=once),
            pl.BlockSpec((d_ff, d), const, pipeline_mode=once),
        ],
        out_specs=pl.BlockSpec((tm, d), tile),
        out_shape=jax.ShapeDtypeStruct((n, d), F32),
        compiler_params=pltpu.CompilerParams(
            dimension_semantics=("arbitrary",), vmem_limit_bytes=VMEM_LIMIT),
        name="finish",
    )(x, a, py, wo_a, wo_p, g2, w_up, w_down)


def _rope_tables(pos):
    half = HEAD_DIM // 2
    inv = ROPE_THETA ** (-jnp.arange(0, HEAD_DIM, 2, dtype=F32) / HEAD_DIM)
    ang = pos.astype(F32)[:, None] * inv[None, :]
    reps = LANES // half
    sign = jnp.tile(jnp.concatenate([-jnp.ones((half,), F32), jnp.ones((half,), F32)]),
                    LANES // HEAD_DIM)
    return jnp.tile(jnp.cos(ang), (1, reps)), jnp.tile(jnp.sin(ang), (1, reps)) * sign[None, :]


def kernel(x_prompt, x_sample, cache_k, cache_v, state_pool, page_table, ln1_g, w_in, q_norm_g, k_norm_g, lambda_q1, lambda_k1, lambda_q2, lambda_k2, subln_g, pool_map, pool_scale, w_out, ln2_g, w_up, w_down):
    b, s, d = x_prompt.shape
    db, n_t, _ = x_sample.shape
    depth = w_in.shape[0]
    n_pool = cache_k.shape[1]
    pw = pool_scale.shape[1]

    idx = jnp.arange(MXU_DIM) // HEAD_DIM
    gsum = (idx[:, None] == idx[None, :]).astype(BF16)
    cos_p, sin_p = _rope_tables(jnp.arange(s))
    cos_s, sin_s = _rope_tables(PAST_LEN + jnp.arange(n_t))
    row = lambda v: v.reshape(1, -1)

    xp = x_prompt
    xs = x_sample.reshape(db * n_t, d)
    outs = [[] for _ in range(6)]
    for l in range(depth):
        w_in_b = w_in[l].astype(BF16)
        gq = row(jnp.tile(q_norm_g[l], MXU_DIM // HEAD_DIM))
        gk = row(jnp.tile(k_norm_g[l], MXU_DIM // HEAD_DIM))
        pm = pool_map[l].astype(BF16)
        ps = row(pool_scale[l])
        wo = w_out[l].astype(BF16)
        wo_a, wo_p = wo[:ATTN_WIDTH], wo[ATTN_WIDTH:]
        wu = w_up[l].astype(BF16)
        wd = w_down[l].astype(BF16)
        lams = [row(v[l]) for v in (lambda_q1, lambda_k1, lambda_q2, lambda_k2)]
        sg = row(subln_g[l])

        q, k, v, py, tail = _inproj_prompt(xp, row(ln1_g[l]), w_in_b, gq, gk, gsum,
                                           cos_p, sin_p, pm, ps)
        a = _attn_prompt(q, k, v, *lams, sg, l)
        xp = _finish(xp.reshape(b * s, d), a.reshape(b * s, ATTN_WIDTH), py.reshape(b * s, pw),
                     wo_a, wo_p, row(ln2_g[l]), wu, wd).reshape(b, s, d)
        outs[0].append(k.reshape(b, s, N_HEADS, V_DIM))
        outs[1].append(v.reshape(b, s, N_HEADS, V_DIM))
        outs[2].append(tail)

        qs, ks, vs, pys, p_new = _inproj_sample(
            xs.reshape(db, n_t * d), row(ln1_g[l]), w_in_b, gq, gk, gsum, cos_s, sin_s, pm, ps,
            state_pool[l].reshape(db, POOL_BUF * pw), n_t, PAST_LEN)
        a_s = _attn_sample(qs.reshape(db, n_t, QK_WIDTH), ks.reshape(db, n_t, QK_WIDTH),
                           vs.reshape(db, n_t, ATTN_WIDTH),
                           cache_k.reshape(-1, V_DIM), cache_v.reshape(-1, V_DIM),
                           page_table, *lams, sg, l, l * n_pool)
        xs = _finish(xs, a_s.reshape(db * n_t, ATTN_WIDTH), pys.reshape(db * n_t, pw),
                     wo_a, wo_p, row(ln2_g[l]), wu, wd)
        outs[3].append(ks.reshape(db, n_t, N_HEADS, V_DIM))
        outs[4].append(vs.reshape(db, n_t, N_HEADS, V_DIM))
        seq_tail = jnp.concatenate([state_pool[l], p_new.reshape(db, n_t, pw)], axis=1)
        outs[5].append(seq_tail[:, -POOL_BUF:])

    return (xp, xs.reshape(db, n_t, d)) + tuple(jnp.stack(o) for o in outs)
```

```python
import functools
import math

import jax
import jax.numpy as jnp
from jax import lax
from jax.experimental import pallas as pl
from jax.experimental.pallas import tpu as pltpu

F32 = jnp.float32
BF16 = jnp.bfloat16

N_HEADS = 4
HEAD_DIM = 64
V_DIM = 2 * HEAD_DIM
QK_WIDTH = N_HEADS * 2 * HEAD_DIM
ATTN_WIDTH = N_HEADS * V_DIM
POOL_WINDOWS = (2, 4, 8, 16)
POOL_BUF = max(POOL_WINDOWS) - 1
ROPE_THETA = 10000.0
EPS = 1e-6
NEG = -1e30
Q_SCALE = HEAD_DIM ** -0.5 * math.log2(math.e)
PAST_LEN = 8192
PAGE_SIZE = 128

LANES = 128
SUBLANES = 8
MXU_DIM = 256
VMEM_LIMIT = 56 * 1024 * 1024

HALO = 2 * SUBLANES
TM_PROJ = 1024
TQ_ATTN = 512
ROW_BLOCK = 256
TM_FIN = 512
FF_CHUNK = 1024
PAGES_PER_STEP = 16


def _lambda_init(layer):
    return 0.8 - 0.6 * math.exp(-0.3 * layer)


def _rms(x, g):
    ms = jnp.mean(x * x, axis=-1, keepdims=True)
    return x * lax.rsqrt(ms + EPS) * g


def _qk_norm_rope(z, gain, gsum, cos, sin):
    ss = jnp.dot((z * z).astype(BF16), gsum, preferred_element_type=F32)
    zn = z * lax.rsqrt(ss * (1.0 / HEAD_DIM) + EPS) * gain
    lane = lax.broadcasted_iota(jnp.int32, (1, LANES), 1)
    first_half = (lane % HEAD_DIM) < (HEAD_DIM // 2)
    outs = []
    for c in range(z.shape[1] // LANES):
        zc = zn[:, c * LANES:(c + 1) * LANES]
        fwd = pltpu.roll(zc, LANES - HEAD_DIM // 2, 1)
        bwd = pltpu.roll(zc, HEAD_DIM // 2, 1)
        outs.append(zc * cos + jnp.where(first_half, fwd, bwd) * sin)
    return outs


def _pool_diffs(hist_rows, p, inv_cnt):
    diffs = []
    for g, w in enumerate(POOL_WINDOWS):
        acc = p[:, g * LANES:(g + 1) * LANES]
        for j in range(1, w):
            acc = acc + hist_rows(j, g)
        diffs.append(acc * inv_cnt[g] - p[:, g * LANES:(g + 1) * LANES])
    return diffs


def _pool_project(diffs, pm_ref, ps_ref):
    outs = []
    for g in range(len(POOL_WINDOWS)):
        y = jnp.dot(diffs[g].astype(BF16), pm_ref[g], preferred_element_type=F32)
        outs.append(y * ps_ref[:, g * LANES:(g + 1) * LANES])
    return outs


def _inproj_prompt_kernel(x_ref, g1_ref, w_ref, gq_ref, gk_ref, gsum_ref, cos_ref, sin_ref,
                          pm_ref, ps_ref,
                          q_ref, k_ref, v_ref, py_ref, tail_ref, pbuf):
    i = pl.program_id(1)
    tm = x_ref.shape[0]
    pw = pbuf.shape[1]

    @pl.when(i == 0)
    def _():
        pbuf[0:HALO, :] = jnp.zeros((HALO, pw), F32)

    @pl.when(i > 0)
    def _():
        pbuf[0:HALO, :] = pbuf[tm:tm + HALO, :]

    xn = _rms(x_ref[...], g1_ref[...]).astype(BF16)
    cos = cos_ref[...]
    sin = sin_ref[...]
    gsum = gsum_ref[...]

    n_blk = QK_WIDTH // MXU_DIM
    zq = [jnp.dot(xn, w_ref[:, blk * MXU_DIM:(blk + 1) * MXU_DIM], preferred_element_type=F32)
          for blk in range(n_blk)]
    zk = [jnp.dot(xn, w_ref[:, QK_WIDTH + blk * MXU_DIM:QK_WIDTH + (blk + 1) * MXU_DIM],
                  preferred_element_type=F32) for blk in range(n_blk)]
    v = jnp.dot(xn, w_ref[:, 2 * QK_WIDTH:2 * QK_WIDTH + ATTN_WIDTH], preferred_element_type=F32)
    p = jnp.dot(xn, w_ref[:, 2 * QK_WIDTH + ATTN_WIDTH:], preferred_element_type=F32)

    for blk in range(n_blk):
        for c, o in enumerate(_qk_norm_rope(zq[blk], gq_ref[...], gsum, cos, sin)):
            col = blk * MXU_DIM + c * LANES
            q_ref[:, col:col + LANES] = (o * Q_SCALE).astype(q_ref.dtype)
    heads_per_blk = MXU_DIM // V_DIM
    for blk in range(n_blk):
        for c, o in enumerate(_qk_norm_rope(zk[blk], gk_ref[...], gsum, cos, sin)):
            k_ref[pl.ds(blk * heads_per_blk + c, tm, stride=N_HEADS), :] = o
    for h in range(N_HEADS):
        v_ref[pl.ds(h, tm, stride=N_HEADS), :] = v[:, h * V_DIM:(h + 1) * V_DIM]

    pbuf[HALO:HALO + tm, :] = p

    pos = i * tm + lax.broadcasted_iota(jnp.int32, (tm, 1), 0)
    inv_cnt = [1.0 / jnp.minimum(pos + 1, w).astype(F32) for w in POOL_WINDOWS]

    def hist_rows(j, g):
        return pbuf[HALO - j:HALO - j + tm, g * LANES:(g + 1) * LANES]

    diffs = _pool_diffs(hist_rows, p, inv_cnt)
    for g, y in enumerate(_pool_project(diffs, pm_ref, ps_ref)):
        py_ref[:, g * LANES:(g + 1) * LANES] = y.astype(py_ref.dtype)

    @pl.when(i == pl.num_programs(1) - 1)
    def _():
        tail_ref[...] = pbuf[HALO + tm - POOL_BUF:HALO + tm, :]


def _inproj_prompt(x, g1, w_in, gq, gk, gsum, cos, sin, pm, ps):
    b, s, d = x.shape
    tm = TM_PROJ
    assert s % tm == 0
    in_w = w_in.shape[1]
    pw = in_w - 2 * QK_WIDTH - ATTN_WIDTH
    const2 = lambda bi, i: (0, 0)
    tile = lambda bi, i: (bi, i, 0)
    return pl.pallas_call(
        _inproj_prompt_kernel,
        grid=(b, s // tm),
        in_specs=[
            pl.BlockSpec((None, tm, d), tile),
            pl.BlockSpec((1, d), const2),
            pl.BlockSpec((d, in_w), const2),
            pl.BlockSpec((1, MXU_DIM), const2),
            pl.BlockSpec((1, MXU_DIM), const2),
            pl.BlockSpec((MXU_DIM, MXU_DIM), const2),
            pl.BlockSpec((tm, LANES), lambda bi, i: (i, 0)),
            pl.BlockSpec((tm, LANES), lambda bi, i: (i, 0)),
            pl.BlockSpec((len(POOL_WINDOWS), LANES, LANES), lambda bi, i: (0, 0, 0)),
            pl.BlockSpec((1, pw), const2),
        ],
        out_specs=[
            pl.BlockSpec((None, tm, QK_WIDTH), tile),
            pl.BlockSpec((tm * N_HEADS, V_DIM), lambda bi, i: (bi * (s // tm) + i, 0)),
            pl.BlockSpec((tm * N_HEADS, V_DIM), lambda bi, i: (bi * (s // tm) + i, 0)),
            pl.BlockSpec((None, tm, pw), tile),
            pl.BlockSpec((None, POOL_BUF, pw), lambda bi, i: (bi, 0, 0)),
        ],
        out_shape=[
            jax.ShapeDtypeStruct((b, s, QK_WIDTH), BF16),
            jax.ShapeDtypeStruct((b * s * N_HEADS, V_DIM), F32),
            jax.ShapeDtypeStruct((b * s * N_HEADS, V_DIM), F32),
            jax.ShapeDtypeStruct((b, s, pw), BF16),
            jax.ShapeDtypeStruct((b, POOL_BUF, pw), F32),
        ],
        scratch_shapes=[pltpu.VMEM((HALO + tm, pw), F32)],
        compiler_params=pltpu.CompilerParams(
            dimension_semantics=("arbitrary", "arbitrary"), vmem_limit_bytes=VMEM_LIMIT),
        name="inproj_prompt",
    )(x, g1, w_in, gq, gk, gsum, cos, sin, pm, ps)


def _inproj_sample_kernel(past_len, x_ref, g1_ref, w_ref, gq_ref, gk_ref, gsum_ref, cos_ref,
                          sin_ref, pm_ref, ps_ref, state_ref,
                          q_ref, k_ref, v_ref, py_ref, p_ref, hist):
    t = pl.program_id(0)
    pw = p_ref.shape[1]
    xn = _rms(x_ref[...], g1_ref[...]).astype(BF16)
    cos = cos_ref[pl.ds(t, 1), :]
    sin = sin_ref[pl.ds(t, 1), :]
    gsum = gsum_ref[...]

    for blk in range(QK_WIDTH // MXU_DIM):
        zq = jnp.dot(xn, w_ref[:, blk * MXU_DIM:(blk + 1) * MXU_DIM], preferred_element_type=F32)
        for c, o in enumerate(_qk_norm_rope(zq, gq_ref[...], gsum, cos, sin)):
            col = blk * MXU_DIM + c * LANES
            q_ref[:, col:col + LANES] = (o * Q_SCALE).astype(q_ref.dtype)
    for blk in range(QK_WIDTH // MXU_DIM):
        c0 = QK_WIDTH + blk * MXU_DIM
        zk = jnp.dot(xn, w_ref[:, c0:c0 + MXU_DIM], preferred_element_type=F32)
        for c, o in enumerate(_qk_norm_rope(zk, gk_ref[...], gsum, cos, sin)):
            col = blk * MXU_DIM + c * LANES
            k_ref[:, col:col + LANES] = o
    v_ref[...] = jnp.dot(xn, w_ref[:, 2 * QK_WIDTH:2 * QK_WIDTH + ATTN_WIDTH],
                         preferred_element_type=F32)
    p = jnp.dot(xn, w_ref[:, 2 * QK_WIDTH + ATTN_WIDTH:], preferred_element_type=F32)
    p_ref[...] = p

    db = p.shape[0]

    @pl.when(t == 0)
    def _():
        for s in range(POOL_BUF):
            hist[s * db:(s + 1) * db, :] = state_ref[:, s * pw:(s + 1) * pw]

    hist[pl.ds(pl.multiple_of((POOL_BUF + t) * db, SUBLANES), db), :] = p

    pos1 = jnp.full((1, LANES), past_len + 1, jnp.int32) + t
    inv_cnt = [1.0 / jnp.minimum(pos1, w).astype(F32) for w in POOL_WINDOWS]

    def hist_rows(j, g):
        start = pl.multiple_of((POOL_BUF + t - j) * db, SUBLANES)
        return hist[pl.ds(start, db), g * LANES:(g + 1) * LANES]

    diffs = _pool_diffs(hist_rows, p, inv_cnt)
    for g, y in enumerate(_pool_project(diffs, pm_ref, ps_ref)):
        py_ref[:, g * LANES:(g + 1) * LANES] = y.astype(py_ref.dtype)


def _inproj_sample(x2, g1, w_in, gq, gk, gsum, cos, sin, pm, ps, state2, n_t, past_len):
    db = x2.shape[0]
    d = x2.shape[1] // n_t
    in_w = w_in.shape[1]
    pw = in_w - 2 * QK_WIDTH - ATTN_WIDTH
    const2 = lambda t: (0, 0)
    step = lambda t: (0, t)
    return pl.pallas_call(
        functools.partial(_inproj_sample_kernel, past_len),
        grid=(n_t,),
        in_specs=[
            pl.BlockSpec((db, d), step),
            pl.BlockSpec((1, d), const2),
            pl.BlockSpec((d, in_w), const2),
            pl.BlockSpec((1, MXU_DIM), const2),
            pl.BlockSpec((1, MXU_DIM), const2),
            pl.BlockSpec((MXU_DIM, MXU_DIM), const2),
            pl.BlockSpec((n_t, LANES), const2),
            pl.BlockSpec((n_t, LANES), const2),
            pl.BlockSpec((len(POOL_WINDOWS), LANES, LANES), lambda t: (0, 0, 0)),
            pl.BlockSpec((1, pw), const2),
            pl.BlockSpec((db, POOL_BUF * pw), const2),
        ],
        out_specs=[
            pl.BlockSpec((db, QK_WIDTH), step),
            pl.BlockSpec((db, QK_WIDTH), step),
            pl.BlockSpec((db, ATTN_WIDTH), step),
            pl.BlockSpec((db, pw), step),
            pl.BlockSpec((db, pw), step),
        ],
        out_shape=[
            jax.ShapeDtypeStruct((db, n_t * QK_WIDTH), BF16),
            jax.ShapeDtypeStruct((db, n_t * QK_WIDTH), F32),
            jax.ShapeDtypeStruct((db, n_t * ATTN_WIDTH), F32),
            jax.ShapeDtypeStruct((db, n_t * pw), BF16),
            jax.ShapeDtypeStruct((db, n_t * pw), F32),
        ],
        scratch_shapes=[pltpu.VMEM(((POOL_BUF + n_t) * db, pw), F32)],
        compiler_params=pltpu.CompilerParams(
            dimension_semantics=("arbitrary",), vmem_limit_bytes=VMEM_LIMIT),
        name="inproj_sample",
    )(x2, g1, w_in, gq, gk, gsum, cos, sin, pm, ps, state2)


def _diff_lambda(lq1_ref, lk1_ref, lq2_ref, lk2_ref, lam0):
    a = jnp.sum(lq1_ref[...] * lk1_ref[...], axis=-1, keepdims=True)
    b = jnp.sum(lq2_ref[...] * lk2_ref[...], axis=-1, keepdims=True)
    return jnp.exp(a) - jnp.exp(b) + lam0


def _split_maps(q):
    lane = lax.broadcasted_iota(jnp.int32, (1, LANES), 1)
    zero = jnp.zeros_like(q)
    return jnp.concatenate([jnp.where(lane < HEAD_DIM, q, zero),
                            jnp.where(lane >= HEAD_DIM, q, zero)], axis=0)


def _subln_out(acc, l, lam, sg, scale):
    m = acc.shape[0] // 2
    o = acc[:m] / l[:m] - lam * (acc[m:] / l[m:])
    return _rms(o, sg) * scale


def _attn_prompt_kernel(lam0, out_scale, lq1_ref, lk1_ref, lq2_ref, lk2_ref, sg_ref,
                        q_ref, k_ref, v_ref, o_ref, kb, vb1, q2, m_s, acc_s):
    s = q_ref.shape[0]
    tq = TQ_ATTN
    rb = ROW_BLOCK
    nq = s // tq
    lam = _diff_lambda(lq1_ref, lk1_ref, lq2_ref, lk2_ref, lam0)
    for h in range(N_HEADS):
        kb[h] = k_ref[pl.ds(h, s, stride=N_HEADS), :].astype(BF16)
        vb1[h, :, 0:V_DIM] = v_ref[pl.ds(h, s, stride=N_HEADS), :].astype(BF16)
        vb1[h, :, V_DIM:] = jnp.ones((s, V_DIM), BF16)

    def chunk(kstart, diagonal):
        for h in range(N_HEADS):
            for r in range(2 * tq // rb):
                rows = slice(r * rb, (r + 1) * rb)
                p0 = (r * rb) % tq
                nk = p0 + rb if diagonal else tq
                sc = lax.dot_general(q2[h, rows, :], kb[h, pl.ds(kstart, nk), :],
                                     (((1,), (1,)), ((), ())), preferred_element_type=F32)
                if diagonal:
                    row = lax.broadcasted_iota(jnp.int32, (rb, rb), 0)
                    col = lax.broadcasted_iota(jnp.int32, (rb, rb), 1)
                    tri = jnp.where(col <= row, sc[:, p0:], NEG)
                    sc = tri if p0 == 0 else jnp.concatenate([sc[:, :p0], tri], axis=1)
                m_prev = m_s[h, rows, :]
                m_new = jnp.maximum(m_prev, jnp.max(sc, axis=-1, keepdims=True))
                alpha = jnp.exp2(m_prev - m_new)
                p = jnp.exp2(sc - jnp.tile(m_new, (1, nk // LANES))).astype(BF16)
                pv = jnp.dot(p, vb1[h, pl.ds(kstart, nk), :], preferred_element_type=F32)
                acc_s[h, rows, :] = jnp.tile(alpha, (1, 2)) * acc_s[h, rows, :] + pv
                m_s[h, rows, :] = m_new

    def q_block(qi, carry):
        qstart = pl.multiple_of(qi * tq, tq)
        for h in range(N_HEADS):
            q2[h] = _split_maps(q_ref[pl.ds(qstart, tq), h * V_DIM:(h + 1) * V_DIM])
        m_s[...] = jnp.full(m_s.shape, NEG, F32)
        acc_s[...] = jnp.zeros(acc_s.shape, F32)

        def kv_block(kj, c):
            chunk(pl.multiple_of(kj * tq, tq), False)
            return c

        lax.fori_loop(0, qi, kv_block, 0)
        chunk(qstart, True)
        for h in range(N_HEADS):
            acc = acc_s[h]
            o = _subln_out(acc[:, 0:V_DIM], acc[:, V_DIM:], lam, sg_ref[...], out_scale)
            o_ref[pl.ds(qstart, tq), h * V_DIM:(h + 1) * V_DIM] = o.astype(o_ref.dtype)
        return carry

    lax.fori_loop(0, nq, q_block, 0)


def _attn_prompt(q, k2, v2, lq1, lk1, lq2, lk2, sg, layer):
    b, s, _ = q.shape
    tq = TQ_ATTN
    assert s % tq == 0 and tq % ROW_BLOCK == 0
    vec = lambda bi: (0, 0)
    kern = functools.partial(_attn_prompt_kernel, _lambda_init(layer), 1.0 - _lambda_init(layer))
    return pl.pallas_call(
        kern,
        grid=(b,),
        in_specs=[
            pl.BlockSpec((1, HEAD_DIM), vec), pl.BlockSpec((1, HEAD_DIM), vec),
            pl.BlockSpec((1, HEAD_DIM), vec), pl.BlockSpec((1, HEAD_DIM), vec),
            pl.BlockSpec((1, V_DIM), vec),
            pl.BlockSpec((None, s, ATTN_WIDTH), lambda bi: (bi, 0, 0)),
            pl.BlockSpec((s * N_HEADS, V_DIM), lambda bi: (bi, 0)),
            pl.BlockSpec((s * N_HEADS, V_DIM), lambda bi: (bi, 0)),
        ],
        out_specs=pl.BlockSpec((None, s, ATTN_WIDTH), lambda bi: (bi, 0, 0)),
        out_shape=jax.ShapeDtypeStruct((b, s, ATTN_WIDTH), BF16),
        scratch_shapes=[
            pltpu.VMEM((N_HEADS, s, V_DIM), BF16), pltpu.VMEM((N_HEADS, s, 2 * V_DIM), BF16),
            pltpu.VMEM((N_HEADS, 2 * tq, V_DIM), BF16),
            pltpu.VMEM((N_HEADS, 2 * tq, LANES), F32),
            pltpu.VMEM((N_HEADS, 2 * tq, 2 * V_DIM), F32),
        ],
        compiler_params=pltpu.CompilerParams(
            dimension_semantics=("arbitrary",), vmem_limit_bytes=VMEM_LIMIT),
        name="attn_prompt",
    )(lq1, lk1, lq2, lk2, sg, q, k2, v2)


def _finish_kernel(x_ref, a_ref, py_ref, woa_ref, wop_ref, g2_ref, wu_ref, wd_ref, y_ref):
    x1 = (x_ref[...]
          + jnp.dot(a_ref[...], woa_ref[...], preferred_element_type=F32)
          + jnp.dot(py_ref[...], wop_ref[...], preferred_element_type=F32))
    h = _rms(x1, g2_ref[...]).astype(BF16)
    y_ref[...] = x1
    for c in range(wu_ref.shape[1] // FF_CHUNK):
        u = jnp.dot(h, wu_ref[:, c * FF_CHUNK:(c + 1) * FF_CHUNK], preferred_element_type=F32)
        u = jnp.square(jnp.maximum(u, 0.0)).astype(BF16)
        y_ref[...] += jnp.dot(u, wd_ref[c * FF_CHUNK:(c + 1) * FF_CHUNK, :],
                              preferred_element_type=F32)


def _finish(x, a, py, wo_a, wo_p, g2, w_up, w_down):
    n, d = x.shape
    tm = min(TM_FIN, n)
    assert n % tm == 0
    d_ff = w_up.shape[1]
    tile = lambda i: (i, 0)
    const = lambda i: (0, 0)
    once = pl.Buffered(1)
    return pl.pallas_call(
        _finish_kernel,
        grid=(n // tm,),
        in_specs=[
            pl.BlockSpec((tm, d), tile),
            pl.BlockSpec((tm, a.shape[1]), tile),
            pl.BlockSpec((tm, py.shape[1]), tile),
            pl.BlockSpec(wo_a.shape, const, pipeline_mode=once),
            pl.BlockSpec(wo_p.shape, const, pipeline_mode=once),
            pl.BlockSpec((1, d), const),
            pl.BlockSpec((d, d_ff), const, pipeline_mode=once),
            pl.BlockSpec((d_ff, d), const, pipeline_mode=once),
        ],
        out_specs=pl.BlockSpec((tm, d), tile),
        out_shape=jax.ShapeDtypeStruct((n, d), F32),
        compiler_params=pltpu.CompilerParams(
            dimension_semantics=("arbitrary",), vmem_limit_bytes=VMEM_LIMIT),
        name="finish",
    )(x, a, py, wo_a, wo_p, g2, w_up, w_down)


def _finish_decode_kernel(lam0, out_scale, n_pages, page_base, seqs,
                          pt_ref,
                          x_ref, a_ref, py_ref, woa_ref, wop_ref, g2_ref, wu_ref, wd_ref,
                          lq1_ref, lk1_ref, lq2_ref, lk2_ref, sg_ref, q_ref, kn_ref, vn_ref,
                          ck_ref, cv_ref,
                          y_ref, o_ref,
                          kbuf, vbuf, sem, m_s, l_s, acc_s, knp, vnp, qbd):
    i = pl.program_id(0)
    n_steps = pl.num_programs(0)
    pps = PAGES_PER_STEP
    n_groups = n_pages // pps
    units = seqs * n_groups
    page_rows = PAGE_SIZE * N_HEADS
    n_t = q_ref.shape[1]
    n_ff = wu_ref.shape[1] // FF_CHUNK

    def page_copies(seq, g, slot):
        out = []
        for j in range(pps):
            page = page_base + pt_ref[seq * n_pages + g * pps + j]
            src = pl.ds(pl.multiple_of(page * page_rows, page_rows), page_rows)
            dst = pl.ds(j * page_rows, page_rows)
            out.append(pltpu.make_async_copy(ck_ref.at[src, :], kbuf.at[slot, dst, :], sem.at[slot]))
            out.append(pltpu.make_async_copy(cv_ref.at[src, :], vbuf.at[slot, dst, :], sem.at[slot]))
        return out

    def start_unit(step, u):
        for cp in page_copies(step * seqs + u // n_groups, u % n_groups, u % 2):
            cp.start()

    def wait_unit(step, u):
        for cp in page_copies(step * seqs + u // n_groups, u % n_groups, u % 2):
            cp.wait()

    def update(h, sc, vv):
        m_prev = m_s[h]
        m_new = jnp.maximum(m_prev, jnp.max(sc, axis=-1, keepdims=True))
        alpha = jnp.exp2(m_prev - m_new)
        p = jnp.exp2(sc - m_new[:, 0:1])
        l_s[h] = alpha * l_s[h] + jnp.sum(p, axis=-1, keepdims=True)
        acc_s[h] = alpha * acc_s[h] + jnp.dot(p.astype(BF16), vv, preferred_element_type=F32)
        m_s[h] = m_new

    def head_rows(buf, slot, h):
        return jnp.concatenate(
            [buf[slot, pl.ds(j * page_rows + h, PAGE_SIZE, stride=N_HEADS), :].astype(BF16)
             for j in range(pps)], axis=0)

    def scores(s_loc, g, slot):
        if g == 0:
            m_s[...] = jnp.full(m_s.shape, NEG, F32)
            l_s[...] = jnp.zeros(l_s.shape, F32)
            acc_s[...] = jnp.zeros(acc_s.shape, F32)
            qbd[...] = jnp.zeros(qbd.shape, qbd.dtype)
            for h in range(N_HEADS):
                cols = slice(h * V_DIM, (h + 1) * V_DIM)
                qbd[h * 2 * n_t:(h + 1) * 2 * n_t, cols] = _split_maps(q_ref[s_loc, :, cols])
        k_all = jnp.concatenate([head_rows(kbuf, slot, h) for h in range(N_HEADS)], axis=1)
        sc_t = lax.dot_general(k_all, qbd[...], (((1,), (1,)), ((), ())),
                               preferred_element_type=F32)
        return sc_t.T

    def absorb(s_loc, g, slot, sc_all):
        for h in range(N_HEADS):
            update(h, sc_all[h * 2 * n_t:(h + 1) * 2 * n_t, :], head_rows(vbuf, slot, h))
        if g == n_groups - 1:
            lam = _diff_lambda(lq1_ref, lk1_ref, lq2_ref, lk2_ref, lam0)
            knp[...] = jnp.zeros(knp.shape, knp.dtype)
            vnp[...] = jnp.zeros(vnp.shape, vnp.dtype)
            knp[0:n_t, :] = kn_ref[s_loc].astype(BF16)
            vnp[0:n_t, :] = vn_ref[s_loc].astype(BF16)
            row = lax.broadcasted_iota(jnp.int32, (2 * n_t, PAGE_SIZE), 0) % n_t
            col = lax.broadcasted_iota(jnp.int32, (2 * n_t, PAGE_SIZE), 1)
            for h in range(N_HEADS):
                cols = slice(h * V_DIM, (h + 1) * V_DIM)
                sc = lax.dot_general(_split_maps(q_ref[s_loc, :, cols]), knp[:, cols],
                                     (((1,), (1,)), ((), ())), preferred_element_type=F32)
                update(h, jnp.where(col <= row, sc, NEG), vnp[:, cols])
                o = _subln_out(acc_s[h], l_s[h], lam, sg_ref[...], out_scale)
                o_ref[s_loc, :, cols] = o.astype(o_ref.dtype)

    @pl.when(i == 0)
    def _():
        start_unit(i, 0)

    x1 = (x_ref[...]
          + jnp.dot(a_ref[...], woa_ref[...], preferred_element_type=F32)
          + jnp.dot(py_ref[...], wop_ref[...], preferred_element_type=F32))
    h_act = _rms(x1, g2_ref[...]).astype(BF16)
    y_ref[...] = x1

    ff_done = 0
    for u in range(units):
        if u + 1 < units:
            start_unit(i, u + 1)
        else:
            @pl.when(i + 1 < n_steps)
            def _():
                start_unit(i + 1, 0)
        wait_unit(i, u)
        ff_upto = (u + 1) * n_ff // units
        sc_all = scores(u // n_groups, u % n_groups, u % 2)
        ups = [jnp.dot(h_act, wu_ref[:, c * FF_CHUNK:(c + 1) * FF_CHUNK],
                       preferred_element_type=F32) for c in range(ff_done, ff_upto)]
        absorb(u // n_groups, u % n_groups, u % 2, sc_all)
        for c, uu in zip(range(ff_done, ff_upto), ups):
            uu = jnp.square(jnp.maximum(uu, 0.0)).astype(BF16)
            y_ref[...] += jnp.dot(uu, wd_ref[c * FF_CHUNK:(c + 1) * FF_CHUNK, :],
                                  preferred_element_type=F32)
        ff_done = ff_upto


def _finish_decode(x, a, py, wo_a, wo_p, g2, w_up, w_down,
                   q3, k3, v3, cache_k2, cache_v2, page_table, lq1, lk1, lq2, lk2, sg, layer,
                   page_base):
    n, d = x.shape
    tm = min(TM_FIN, n)
    n_steps = n // tm
    db, n_t, width = q3.shape
    n_pages = page_table.shape[1]
    pps = PAGES_PER_STEP
    d_ff = w_up.shape[1]
    assert n % tm == 0 and db % n_steps == 0 and n_pages % pps == 0
    seqs = db // n_steps
    assert (seqs * (n_pages // pps)) % 2 == 0
    tile = lambda i, pt: (i, 0)
    const = lambda i, pt: (0, 0)
    seq = lambda i, pt: (i, 0, 0)
    once = pl.Buffered(1)
    page_rows = PAGE_SIZE * N_HEADS
    kern = functools.partial(_finish_decode_kernel, _lambda_init(layer), 1.0 - _lambda_init(layer),
                             n_pages, page_base, seqs)
    grid_spec = pltpu.PrefetchScalarGridSpec(
        num_scalar_prefetch=1,
        grid=(n_steps,),
        in_specs=[
            pl.BlockSpec((tm, d), tile),
            pl.BlockSpec((tm, a.shape[1]), tile),
            pl.BlockSpec((tm, py.shape[1]), tile),
            pl.BlockSpec(wo_a.shape, const, pipeline_mode=once),
            pl.BlockSpec(wo_p.shape, const, pipeline_mode=once),
            pl.BlockSpec((1, d), const),
            pl.BlockSpec((d, d_ff), const, pipeline_mode=once),
            pl.BlockSpec((d_ff, d), const, pipeline_mode=once),
            pl.BlockSpec((1, HEAD_DIM), const), pl.BlockSpec((1, HEAD_DIM), const),
            pl.BlockSpec((1, HEAD_DIM), const), pl.BlockSpec((1, HEAD_DIM), const),
            pl.BlockSpec((1, V_DIM), const),
            pl.BlockSpec((seqs, n_t, width), seq),
            pl.BlockSpec((seqs, n_t, width), seq),
            pl.BlockSpec((seqs, n_t, width), seq),
            pl.BlockSpec(memory_space=pl.ANY),
            pl.BlockSpec(memory_space=pl.ANY),
        ],
        out_specs=[
            pl.BlockSpec((tm, d), tile),
            pl.BlockSpec((seqs, n_t, width), seq),
        ],
        scratch_shapes=[
            pltpu.VMEM((2, pps * page_rows, V_DIM), F32),
            pltpu.VMEM((2, pps * page_rows, V_DIM), F32),
            pltpu.SemaphoreType.DMA((2,)),
            pltpu.VMEM((N_HEADS, 2 * n_t, LANES), F32), pltpu.VMEM((N_HEADS, 2 * n_t, LANES), F32),
            pltpu.VMEM((N_HEADS, 2 * n_t, V_DIM), F32),
            pltpu.VMEM((PAGE_SIZE, width), BF16), pltpu.VMEM((PAGE_SIZE, width), BF16),
            pltpu.VMEM((LANES, width), BF16),
        ],
    )
    assert N_HEADS * 2 * n_t <= LANES
    return pl.pallas_call(
        kern,
        grid_spec=grid_spec,
        out_shape=[jax.ShapeDtypeStruct((n, d), F32),
                   jax.ShapeDtypeStruct((db, n_t, width), BF16)],
        compiler_params=pltpu.CompilerParams(
            dimension_semantics=("arbitrary",), vmem_limit_bytes=VMEM_LIMIT),
        name="finish_decode",
    )(page_table.reshape(-1), x, a, py, wo_a, wo_p, g2, w_up, w_down,
      lq1, lk1, lq2, lk2, sg, q3, k3, v3, cache_k2, cache_v2)


def _rope_tables(pos):
    half = HEAD_DIM // 2
    inv = ROPE_THETA ** (-jnp.arange(0, HEAD_DIM, 2, dtype=F32) / HEAD_DIM)
    ang = pos.astype(F32)[:, None] * inv[None, :]
    reps = LANES // half
    sign = jnp.tile(jnp.concatenate([-jnp.ones((half,), F32), jnp.ones((half,), F32)]),
                    LANES // HEAD_DIM)
    return jnp.tile(jnp.cos(ang), (1, reps)), jnp.tile(jnp.sin(ang), (1, reps)) * sign[None, :]


def kernel(x_prompt, x_sample, cache_k, cache_v, state_pool, page_table, ln1_g, w_in, q_norm_g, k_norm_g, lambda_q1, lambda_k1, lambda_q2, lambda_k2, subln_g, pool_map, pool_scale, w_out, ln2_g, w_up, w_down):
    b, s, d = x_prompt.shape
    db, n_t, _ = x_sample.shape
    depth = w_in.shape[0]
    n_pool = cache_k.shape[1]
    pw = pool_scale.shape[1]

    idx = jnp.arange(MXU_DIM) // HEAD_DIM
    gsum = (idx[:, None] == idx[None, :]).astype(BF16)
    cos_p, sin_p = _rope_tables(jnp.arange(s))
    cos_s, sin_s = _rope_tables(PAST_LEN + jnp.arange(n_t))
    row = lambda v: v.reshape(1, -1)

    xp = x_prompt
    xs = x_sample.reshape(db * n_t, d)
    outs = [[] for _ in range(6)]
    for l in range(depth):
        w_in_b = w_in[l].astype(BF16)
        gq = row(jnp.tile(q_norm_g[l], MXU_DIM // HEAD_DIM))
        gk = row(jnp.tile(k_norm_g[l], MXU_DIM // HEAD_DIM))
        pm = pool_map[l].astype(BF16)
        ps = row(pool_scale[l])
        wo = w_out[l].astype(BF16)
        wo_a, wo_p = wo[:ATTN_WIDTH], wo[ATTN_WIDTH:]
        wu = w_up[l].astype(BF16)
        wd = w_down[l].astype(BF16)
        lams = [row(v[l]) for v in (lambda_q1, lambda_k1, lambda_q2, lambda_k2)]
        sg = row(subln_g[l])

        q, k, v, py, tail = _inproj_prompt(xp, row(ln1_g[l]), w_in_b, gq, gk, gsum,
                                           cos_p, sin_p, pm, ps)
        a = _attn_prompt(q, k, v, *lams, sg, l)
        qs, ks, vs, pys, p_new = _inproj_sample(
            xs.reshape(db, n_t * d), row(ln1_g[l]), w_in_b, gq, gk, gsum, cos_s, sin_s, pm, ps,
            state_pool[l].reshape(db, POOL_BUF * pw), n_t, PAST_LEN)

        xp, a_s = _finish_decode(
            xp.reshape(b * s, d), a.reshape(b * s, ATTN_WIDTH), py.reshape(b * s, pw),
            wo_a, wo_p, row(ln2_g[l]), wu, wd,
            qs.reshape(db, n_t, QK_WIDTH), ks.reshape(db, n_t, QK_WIDTH),
            vs.reshape(db, n_t, ATTN_WIDTH),
            cache_k.reshape(-1, V_DIM), cache_v.reshape(-1, V_DIM),
            page_table, *lams, sg, l, l * n_pool)
        xp = xp.reshape(b, s, d)
        outs[0].append(k.reshape(b, s, N_HEADS, V_DIM))
        outs[1].append(v.reshape(b, s, N_HEADS, V_DIM))
        outs[2].append(tail)
        xs = _finish(xs, a_s.reshape(db * n_t, ATTN_WIDTH), pys.reshape(db * n_t, pw),
                     wo_a, wo_p, row(ln2_g[l]), wu, wd)
        outs[3].append(ks.reshape(db, n_t, N_HEADS, V_DIM))
        outs[4].append(vs.reshape(db, n_t, N_HEADS, V_DIM))
        seq_tail = jnp.concatenate([state_pool[l], p_new.reshape(db, n_t, pw)], axis=1)
        outs[5].append(seq_tail[:, -POOL_BUF:])

    return (xp, xs.reshape(db, n_t, d)) + tuple(jnp.stack(o) for o in outs)
```

```python
import functools
import math

import jax
import jax.numpy as jnp
from jax import lax
from jax.experimental import pallas as pl
from jax.experimental.pallas import tpu as pltpu

F32 = jnp.float32
BF16 = jnp.bfloat16

N_HEADS = 4
HEAD_DIM = 64
V_DIM = 2 * HEAD_DIM
QK_WIDTH = N_HEADS * 2 * HEAD_DIM
ATTN_WIDTH = N_HEADS * V_DIM
POOL_WINDOWS = (2, 4, 8, 16)
POOL_BUF = max(POOL_WINDOWS) - 1
ROPE_THETA = 10000.0
EPS = 1e-6
NEG = -1e30
Q_SCALE = HEAD_DIM ** -0.5 * math.log2(math.e)
PAST_LEN = 8192
PAGE_SIZE = 128

LANES = 128
SUBLANES = 8
MXU_DIM = 256
VMEM_LIMIT = 56 * 1024 * 1024

HALO = 2 * SUBLANES
TM_PROJ = 1024
TQ_ATTN = 512
ROW_BLOCK = 256
TM_FIN = 512
FF_CHUNK = 1024
PAGES_PER_STEP = 16


def _lambda_init(layer):
    return 0.8 - 0.6 * math.exp(-0.3 * layer)


def _rms(x, g):
    ms = jnp.mean(x * x, axis=-1, keepdims=True)
    return x * lax.rsqrt(ms + EPS) * g


def _qk_norm_rope(z, gain, gsum, cos, sin):
    ss = jnp.dot((z * z).astype(BF16), gsum, preferred_element_type=F32)
    zn = z * lax.rsqrt(ss * (1.0 / HEAD_DIM) + EPS) * gain
    lane = lax.broadcasted_iota(jnp.int32, (1, LANES), 1)
    first_half = (lane % HEAD_DIM) < (HEAD_DIM // 2)
    outs = []
    for c in range(z.shape[1] // LANES):
        zc = zn[:, c * LANES:(c + 1) * LANES]
        fwd = pltpu.roll(zc, LANES - HEAD_DIM // 2, 1)
        bwd = pltpu.roll(zc, HEAD_DIM // 2, 1)
        outs.append(zc * cos + jnp.where(first_half, fwd, bwd) * sin)
    return outs


def _pool_diffs(hist_rows, p, inv_cnt):
    diffs = []
    for g, w in enumerate(POOL_WINDOWS):
        acc = p[:, g * LANES:(g + 1) * LANES]
        for j in range(1, w):
            acc = acc + hist_rows(j, g)
        diffs.append(acc * inv_cnt[g] - p[:, g * LANES:(g + 1) * LANES])
    return diffs


def _pool_project(diffs, pm_ref, ps_ref):
    outs = []
    for g in range(len(POOL_WINDOWS)):
        y = jnp.dot(diffs[g].astype(BF16), pm_ref[g], preferred_element_type=F32)
        outs.append(y * ps_ref[:, g * LANES:(g + 1) * LANES])
    return outs


def _inproj_prompt_kernel(x_ref, g1_ref, w_ref, gq_ref, gk_ref, gsum_ref, cos_ref, sin_ref,
                          pm_ref, ps_ref,
                          q_ref, k_ref, v_ref, py_ref, tail_ref, pbuf):
    i = pl.program_id(1)
    tm = x_ref.shape[0]
    pw = pbuf.shape[1]

    @pl.when(i == 0)
    def _():
        pbuf[0:HALO, :] = jnp.zeros((HALO, pw), F32)

    @pl.when(i > 0)
    def _():
        pbuf[0:HALO, :] = pbuf[tm:tm + HALO, :]

    xn = _rms(x_ref[...], g1_ref[...]).astype(BF16)
    cos = cos_ref[...]
    sin = sin_ref[...]
    gsum = gsum_ref[...]

    n_blk = QK_WIDTH // MXU_DIM
    zq = [jnp.dot(xn, w_ref[:, blk * MXU_DIM:(blk + 1) * MXU_DIM], preferred_element_type=F32)
          for blk in range(n_blk)]
    zk = [jnp.dot(xn, w_ref[:, QK_WIDTH + blk * MXU_DIM:QK_WIDTH + (blk + 1) * MXU_DIM],
                  preferred_element_type=F32) for blk in range(n_blk)]
    v = jnp.dot(xn, w_ref[:, 2 * QK_WIDTH:2 * QK_WIDTH + ATTN_WIDTH], preferred_element_type=F32)
    p = jnp.dot(xn, w_ref[:, 2 * QK_WIDTH + ATTN_WIDTH:], preferred_element_type=F32)

    for blk in range(n_blk):
        for c, o in enumerate(_qk_norm_rope(zq[blk], gq_ref[...], gsum, cos, sin)):
            col = blk * MXU_DIM + c * LANES
            q_ref[:, col:col + LANES] = (o * Q_SCALE).astype(q_ref.dtype)
    heads_per_blk = MXU_DIM // V_DIM
    for blk in range(n_blk):
        for c, o in enumerate(_qk_norm_rope(zk[blk], gk_ref[...], gsum, cos, sin)):
            k_ref[pl.ds(blk * heads_per_blk + c, tm, stride=N_HEADS), :] = o
    for h in range(N_HEADS):
        v_ref[pl.ds(h, tm, stride=N_HEADS), :] = v[:, h * V_DIM:(h + 1) * V_DIM]

    pbuf[HALO:HALO + tm, :] = p

    pos = i * tm + lax.broadcasted_iota(jnp.int32, (tm, 1), 0)
    inv_cnt = [1.0 / jnp.minimum(pos + 1, w).astype(F32) for w in POOL_WINDOWS]

    def hist_rows(j, g):
        return pbuf[HALO - j:HALO - j + tm, g * LANES:(g + 1) * LANES]

    diffs = _pool_diffs(hist_rows, p, inv_cnt)
    for g, y in enumerate(_pool_project(diffs, pm_ref, ps_ref)):
        py_ref[:, g * LANES:(g + 1) * LANES] = y.astype(py_ref.dtype)

    @pl.when(i == pl.num_programs(1) - 1)
    def _():
        tail_ref[...] = pbuf[HALO + tm - POOL_BUF:HALO + tm, :]


def _inproj_prompt(x, g1, w_in, gq, gk, gsum, cos, sin, pm, ps):
    b, s, d = x.shape
    tm = TM_PROJ
    assert s % tm == 0
    in_w = w_in.shape[1]
    pw = in_w - 2 * QK_WIDTH - ATTN_WIDTH
    const2 = lambda bi, i: (0, 0)
    tile = lambda bi, i: (bi, i, 0)
    return pl.pallas_call(
        _inproj_prompt_kernel,
        grid=(b, s // tm),
        in_specs=[
            pl.BlockSpec((None, tm, d), tile),
            pl.BlockSpec((1, d), const2),
            pl.BlockSpec((d, in_w), const2),
            pl.BlockSpec((1, MXU_DIM), const2),
            pl.BlockSpec((1, MXU_DIM), const2),
            pl.BlockSpec((MXU_DIM, MXU_DIM), const2),
            pl.BlockSpec((tm, LANES), lambda bi, i: (i, 0)),
            pl.BlockSpec((tm, LANES), lambda bi, i: (i, 0)),
            pl.BlockSpec((len(POOL_WINDOWS), LANES, LANES), lambda bi, i: (0, 0, 0)),
            pl.BlockSpec((1, pw), const2),
        ],
        out_specs=[
            pl.BlockSpec((None, tm, QK_WIDTH), tile),
            pl.BlockSpec((tm * N_HEADS, V_DIM), lambda bi, i: (bi * (s // tm) + i, 0)),
            pl.BlockSpec((tm * N_HEADS, V_DIM), lambda bi, i: (bi * (s // tm) + i, 0)),
            pl.BlockSpec((None, tm, pw), tile),
            pl.BlockSpec((None, POOL_BUF, pw), lambda bi, i: (bi, 0, 0)),
        ],
        out_shape=[
            jax.ShapeDtypeStruct((b, s, QK_WIDTH), BF16),
            jax.ShapeDtypeStruct((b * s * N_HEADS, V_DIM), F32),
            jax.ShapeDtypeStruct((b * s * N_HEADS, V_DIM), F32),
            jax.ShapeDtypeStruct((b, s, pw), BF16),
            jax.ShapeDtypeStruct((b, POOL_BUF, pw), F32),
        ],
        scratch_shapes=[pltpu.VMEM((HALO + tm, pw), F32)],
        compiler_params=pltpu.CompilerParams(
            dimension_semantics=("arbitrary", "arbitrary"), vmem_limit_bytes=VMEM_LIMIT),
        name="inproj_prompt",
    )(x, g1, w_in, gq, gk, gsum, cos, sin, pm, ps)


def _inproj_sample_kernel(past_len, x_ref, g1_ref, w_ref, gq_ref, gk_ref, gsum_ref, cos_ref,
                          sin_ref, pm_ref, ps_ref, state_ref,
                          q_ref, k_ref, v_ref, py_ref, p_ref, hist):
    t = pl.program_id(0)
    pw = p_ref.shape[1]
    xn = _rms(x_ref[...], g1_ref[...]).astype(BF16)
    cos = cos_ref[pl.ds(t, 1), :]
    sin = sin_ref[pl.ds(t, 1), :]
    gsum = gsum_ref[...]

    for blk in range(QK_WIDTH // MXU_DIM):
        zq = jnp.dot(xn, w_ref[:, blk * MXU_DIM:(blk + 1) * MXU_DIM], preferred_element_type=F32)
        for c, o in enumerate(_qk_norm_rope(zq, gq_ref[...], gsum, cos, sin)):
            col = blk * MXU_DIM + c * LANES
            q_ref[:, col:col + LANES] = (o * Q_SCALE).astype(q_ref.dtype)
    for blk in range(QK_WIDTH // MXU_DIM):
        c0 = QK_WIDTH + blk * MXU_DIM
        zk = jnp.dot(xn, w_ref[:, c0:c0 + MXU_DIM], preferred_element_type=F32)
        for c, o in enumerate(_qk_norm_rope(zk, gk_ref[...], gsum, cos, sin)):
            col = blk * MXU_DIM + c * LANES
            k_ref[:, col:col + LANES] = o
    v_ref[...] = jnp.dot(xn, w_ref[:, 2 * QK_WIDTH:2 * QK_WIDTH + ATTN_WIDTH],
                         preferred_element_type=F32)
    p = jnp.dot(xn, w_ref[:, 2 * QK_WIDTH + ATTN_WIDTH:], preferred_element_type=F32)
    p_ref[...] = p

    db = p.shape[0]

    @pl.when(t == 0)
    def _():
        for s in range(POOL_BUF):
            hist[s * db:(s + 1) * db, :] = state_ref[:, s * pw:(s + 1) * pw]

    hist[pl.ds(pl.multiple_of((POOL_BUF + t) * db, SUBLANES), db), :] = p

    pos1 = jnp.full((1, LANES), past_len + 1, jnp.int32) + t
    inv_cnt = [1.0 / jnp.minimum(pos1, w).astype(F32) for w in POOL_WINDOWS]

    def hist_rows(j, g):
        start = pl.multiple_of((POOL_BUF + t - j) * db, SUBLANES)
        return hist[pl.ds(start, db), g * LANES:(g + 1) * LANES]

    diffs = _pool_diffs(hist_rows, p, inv_cnt)
    for g, y in enumerate(_pool_project(diffs, pm_ref, ps_ref)):
        py_ref[:, g * LANES:(g + 1) * LANES] = y.astype(py_ref.dtype)


def _inproj_sample(x2, g1, w_in, gq, gk, gsum, cos, sin, pm, ps, state2, n_t, past_len):
    db = x2.shape[0]
    d = x2.shape[1] // n_t
    in_w = w_in.shape[1]
    pw = in_w - 2 * QK_WIDTH - ATTN_WIDTH
    const2 = lambda t: (0, 0)
    step = lambda t: (0, t)
    return pl.pallas_call(
        functools.partial(_inproj_sample_kernel, past_len),
        grid=(n_t,),
        in_specs=[
            pl.BlockSpec((db, d), step),
            pl.BlockSpec((1, d), const2),
            pl.BlockSpec((d, in_w), const2),
            pl.BlockSpec((1, MXU_DIM), const2),
            pl.BlockSpec((1, MXU_DIM), const2),
            pl.BlockSpec((MXU_DIM, MXU_DIM), const2),
            pl.BlockSpec((n_t, LANES), const2),
            pl.BlockSpec((n_t, LANES), const2),
            pl.BlockSpec((len(POOL_WINDOWS), LANES, LANES), lambda t: (0, 0, 0)),
            pl.BlockSpec((1, pw), const2),
            pl.BlockSpec((db, POOL_BUF * pw), const2),
        ],
        out_specs=[
            pl.BlockSpec((db, QK_WIDTH), step),
            pl.BlockSpec((db, QK_WIDTH), step),
            pl.BlockSpec((db, ATTN_WIDTH), step),
            pl.BlockSpec((db, pw), step),
            pl.BlockSpec((db, pw), step),
        ],
        out_shape=[
            jax.ShapeDtypeStruct((db, n_t * QK_WIDTH), BF16),
            jax.ShapeDtypeStruct((db, n_t * QK_WIDTH), F32),
            jax.ShapeDtypeStruct((db, n_t * ATTN_WIDTH), F32),
            jax.ShapeDtypeStruct((db, n_t * pw), BF16),
            jax.ShapeDtypeStruct((db, n_t * pw), F32),
        ],
        scratch_shapes=[pltpu.VMEM(((POOL_BUF + n_t) * db, pw), F32)],
        compiler_params=pltpu.CompilerParams(
            dimension_semantics=("arbitrary",), vmem_limit_bytes=VMEM_LIMIT),
        name="inproj_sample",
    )(x2, g1, w_in, gq, gk, gsum, cos, sin, pm, ps, state2)


def _diff_lambda(lq1_ref, lk1_ref, lq2_ref, lk2_ref, lam0):
    a = jnp.sum(lq1_ref[...] * lk1_ref[...], axis=-1, keepdims=True)
    b = jnp.sum(lq2_ref[...] * lk2_ref[...], axis=-1, keepdims=True)
    return jnp.exp(a) - jnp.exp(b) + lam0


def _split_maps(q):
    lane = lax.broadcasted_iota(jnp.int32, (1, LANES), 1)
    zero = jnp.zeros_like(q)
    return jnp.concatenate([jnp.where(lane < HEAD_DIM, q, zero),
                            jnp.where(lane >= HEAD_DIM, q, zero)], axis=0)


def _subln_out(acc, l, lam, sg, scale):
    m = acc.shape[0] // 2
    o = acc[:m] / l[:m] - lam * (acc[m:] / l[m:])
    return _rms(o, sg) * scale


def _attn_prompt_kernel(lam0, out_scale, lq1_ref, lk1_ref, lq2_ref, lk2_ref, sg_ref,
                        q_ref, k_ref, v_ref, o_ref, kb, vb1, q2, m_s, acc_s):
    s = q_ref.shape[0]
    tq = TQ_ATTN
    rb = ROW_BLOCK
    nq = s // tq
    lam = _diff_lambda(lq1_ref, lk1_ref, lq2_ref, lk2_ref, lam0)
    for h in range(N_HEADS):
        kb[h] = k_ref[pl.ds(h, s, stride=N_HEADS), :].astype(BF16)
        vb1[h, :, 0:V_DIM] = v_ref[pl.ds(h, s, stride=N_HEADS), :].astype(BF16)
        vb1[h, :, V_DIM:] = jnp.ones((s, V_DIM), BF16)

    def chunk(kstart, diagonal):
        for h in range(N_HEADS):
            for r in range(2 * tq // rb):
                rows = slice(r * rb, (r + 1) * rb)
                p0 = (r * rb) % tq
                nk = p0 + rb if diagonal else tq
                sc = lax.dot_general(q2[h, rows, :], kb[h, kstart:kstart + nk, :],
                                     (((1,), (1,)), ((), ())), preferred_element_type=F32)
                if diagonal:
                    row = lax.broadcasted_iota(jnp.int32, (rb, rb), 0)
                    col = lax.broadcasted_iota(jnp.int32, (rb, rb), 1)
                    tri = jnp.where(col <= row, sc[:, p0:], NEG)
                    sc = tri if p0 == 0 else jnp.concatenate([sc[:, :p0], tri], axis=1)
                    m_new = jnp.broadcast_to(jnp.max(sc, axis=-1, keepdims=True), (rb, LANES))
                else:
                    m_prev = m_s[h, rows, :]
                    m_new = jnp.maximum(m_prev, jnp.max(sc, axis=-1, keepdims=True))
                p = jnp.exp2(sc - jnp.tile(m_new, (1, nk // LANES))).astype(BF16)
                pv = jnp.dot(p, vb1[h, kstart:kstart + nk, :], preferred_element_type=F32)
                if diagonal:
                    acc_s[h, rows, :] = pv
                else:
                    alpha = jnp.exp2(m_prev - m_new)
                    acc_s[h, rows, :] = jnp.tile(alpha, (1, 2)) * acc_s[h, rows, :] + pv
                m_s[h, rows, :] = m_new

    for qi in range(nq):
        qstart = qi * tq
        for h in range(N_HEADS):
            q2[h] = _split_maps(q_ref[qstart:qstart + tq, h * V_DIM:(h + 1) * V_DIM])
        chunk(qstart, True)
        for kj in range(qi):
            chunk(kj * tq, False)
        for h in range(N_HEADS):
            acc = acc_s[h]
            o = _subln_out(acc[:, 0:V_DIM], acc[:, V_DIM:], lam, sg_ref[...], out_scale)
            o_ref[qstart:qstart + tq, h * V_DIM:(h + 1) * V_DIM] = o.astype(o_ref.dtype)


def _attn_prompt(q, k2, v2, lq1, lk1, lq2, lk2, sg, layer):
    b, s, _ = q.shape
    tq = TQ_ATTN
    assert s % tq == 0 and tq % ROW_BLOCK == 0
    vec = lambda bi: (0, 0)
    kern = functools.partial(_attn_prompt_kernel, _lambda_init(layer), 1.0 - _lambda_init(layer))
    return pl.pallas_call(
        kern,
        grid=(b,),
        in_specs=[
            pl.BlockSpec((1, HEAD_DIM), vec), pl.BlockSpec((1, HEAD_DIM), vec),
            pl.BlockSpec((1, HEAD_DIM), vec), pl.BlockSpec((1, HEAD_DIM), vec),
            pl.BlockSpec((1, V_DIM), vec),
            pl.BlockSpec((None, s, ATTN_WIDTH), lambda bi: (bi, 0, 0)),
            pl.BlockSpec((s * N_HEADS, V_DIM), lambda bi: (bi, 0)),
            pl.BlockSpec((s * N_HEADS, V_DIM), lambda bi: (bi, 0)),
        ],
        out_specs=pl.BlockSpec((None, s, ATTN_WIDTH), lambda bi: (bi, 0, 0)),
        out_shape=jax.ShapeDtypeStruct((b, s, ATTN_WIDTH), BF16),
        scratch_shapes=[
            pltpu.VMEM((N_HEADS, s, V_DIM), BF16), pltpu.VMEM((N_HEADS, s, 2 * V_DIM), BF16),
            pltpu.VMEM((N_HEADS, 2 * tq, V_DIM), BF16),
            pltpu.VMEM((N_HEADS, 2 * tq, LANES), F32),
            pltpu.VMEM((N_HEADS, 2 * tq, 2 * V_DIM), F32),
        ],
        compiler_params=pltpu.CompilerParams(
            dimension_semantics=("arbitrary",), vmem_limit_bytes=VMEM_LIMIT),
        name="attn_prompt",
    )(lq1, lk1, lq2, lk2, sg, q, k2, v2)


def _finish_kernel(x_ref, a_ref, py_ref, woa_ref, wop_ref, g2_ref, wu_ref, wd_ref, y_ref):
    x1 = (x_ref[...]
          + jnp.dot(a_ref[...], woa_ref[...], preferred_element_type=F32)
          + jnp.dot(py_ref[...], wop_ref[...], preferred_element_type=F32))
    h = _rms(x1, g2_ref[...]).astype(BF16)
    y_ref[...] = x1
    for c in range(wu_ref.shape[1] // FF_CHUNK):
        u = jnp.dot(h, wu_ref[:, c * FF_CHUNK:(c + 1) * FF_CHUNK], preferred_element_type=F32)
        u = jnp.square(jnp.maximum(u, 0.0)).astype(BF16)
        y_ref[...] += jnp.dot(u, wd_ref[c * FF_CHUNK:(c + 1) * FF_CHUNK, :],
                              preferred_element_type=F32)


def _finish(x, a, py, wo_a, wo_p, g2, w_up, w_down):
    n, d = x.shape
    tm = min(TM_FIN, n)
    assert n % tm == 0
    d_ff = w_up.shape[1]
    tile = lambda i: (i, 0)
    const = lambda i: (0, 0)
    once = pl.Buffered(1)
    return pl.pallas_call(
        _finish_kernel,
        grid=(n // tm,),
        in_specs=[
            pl.BlockSpec((tm, d), tile),
            pl.BlockSpec((tm, a.shape[1]), tile),
            pl.BlockSpec((tm, py.shape[1]), tile),
            pl.BlockSpec(wo_a.shape, const, pipeline_mode=once),
            pl.BlockSpec(wo_p.shape, const, pipeline_mode=once),
            pl.BlockSpec((1, d), const),
            pl.BlockSpec((d, d_ff), const, pipeline_mode=once),
            pl.BlockSpec((d_ff, d), const, pipeline_mode=once),
        ],
        out_specs=pl.BlockSpec((tm, d), tile),
        out_shape=jax.ShapeDtypeStruct((n, d), F32),
        compiler_params=pltpu.CompilerParams(
            dimension_semantics=("arbitrary",), vmem_limit_bytes=VMEM_LIMIT),
        name="finish",
    )(x, a, py, wo_a, wo_p, g2, w_up, w_down)


def _finish_decode_kernel(lam0, out_scale, n_pages, page_base, seqs,
                          pt_ref,
                          x_ref, a_ref, py_ref, woa_ref, wop_ref, g2_ref, wu_ref, wd_ref,
                          lq1_ref, lk1_ref, lq2_ref, lk2_ref, sg_ref, q_ref, kn_ref, vn_ref,
                          ck_ref, cv_ref,
                          y_ref, o_ref,
                          kbuf, vbuf, sem, m_s, l_s, acc_s, knp, vnp, qbd):
    i = pl.program_id(0)
    n_steps = pl.num_programs(0)
    pps = PAGES_PER_STEP
    n_groups = n_pages // pps
    units = seqs * n_groups
    page_rows = PAGE_SIZE * N_HEADS
    n_t = q_ref.shape[1]
    n_ff = wu_ref.shape[1] // FF_CHUNK

    def page_copies(seq, g, slot):
        out = []
        for j in range(pps):
            page = page_base + pt_ref[seq * n_pages + g * pps + j]
            src = pl.ds(pl.multiple_of(page * page_rows, page_rows), page_rows)
            dst = pl.ds(j * page_rows, page_rows)
            out.append(pltpu.make_async_copy(ck_ref.at[src, :], kbuf.at[slot, dst, :], sem.at[slot]))
            out.append(pltpu.make_async_copy(cv_ref.at[src, :], vbuf.at[slot, dst, :], sem.at[slot]))
        return out

    def start_unit(step, u):
        for cp in page_copies(step * seqs + u // n_groups, u % n_groups, u % 2):
            cp.start()

    def wait_unit(step, u):
        for cp in page_copies(step * seqs + u // n_groups, u % n_groups, u % 2):
            cp.wait()

    def update(h, sc, vv):
        m_prev = m_s[h]
        m_new = jnp.maximum(m_prev, jnp.max(sc, axis=-1, keepdims=True))
        alpha = jnp.exp2(m_prev - m_new)
        p = jnp.exp2(sc - m_new[:, 0:1])
        l_s[h] = alpha * l_s[h] + jnp.sum(p, axis=-1, keepdims=True)
        acc_s[h] = alpha * acc_s[h] + jnp.dot(p.astype(BF16), vv, preferred_element_type=F32)
        m_s[h] = m_new

    def head_rows(buf, slot, h):
        return jnp.concatenate(
            [buf[slot, pl.ds(j * page_rows + h, PAGE_SIZE, stride=N_HEADS), :].astype(BF16)
             for j in range(pps)], axis=0)

    def scores(s_loc, g, slot):
        if g == 0:
            m_s[...] = jnp.full(m_s.shape, NEG, F32)
            l_s[...] = jnp.zeros(l_s.shape, F32)
            acc_s[...] = jnp.zeros(acc_s.shape, F32)
            qbd[...] = jnp.zeros(qbd.shape, qbd.dtype)
            for h in range(N_HEADS):
                cols = slice(h * V_DIM, (h + 1) * V_DIM)
                qbd[h * 2 * n_t:(h + 1) * 2 * n_t, cols] = _split_maps(q_ref[s_loc, :, cols])
        k_all = jnp.concatenate([head_rows(kbuf, slot, h) for h in range(N_HEADS)], axis=1)
        sc_t = lax.dot_general(k_all, qbd[...], (((1,), (1,)), ((), ())),
                               preferred_element_type=F32)
        return sc_t.T

    def absorb(s_loc, g, slot, sc_all):
        for h in range(N_HEADS):
            update(h, sc_all[h * 2 * n_t:(h + 1) * 2 * n_t, :], head_rows(vbuf, slot, h))
        if g == n_groups - 1:
            lam = _diff_lambda(lq1_ref, lk1_ref, lq2_ref, lk2_ref, lam0)
            knp[...] = jnp.zeros(knp.shape, knp.dtype)
            vnp[...] = jnp.zeros(vnp.shape, vnp.dtype)
            knp[0:n_t, :] = kn_ref[s_loc].astype(BF16)
            vnp[0:n_t, :] = vn_ref[s_loc].astype(BF16)
            row = lax.broadcasted_iota(jnp.int32, (2 * n_t, PAGE_SIZE), 0) % n_t
            col = lax.broadcasted_iota(jnp.int32, (2 * n_t, PAGE_SIZE), 1)
            for h in range(N_HEADS):
                cols = slice(h * V_DIM, (h + 1) * V_DIM)
                sc = lax.dot_general(_split_maps(q_ref[s_loc, :, cols]), knp[:, cols],
                                     (((1,), (1,)), ((), ())), preferred_element_type=F32)
                update(h, jnp.where(col <= row, sc, NEG), vnp[:, cols])
                o = _subln_out(acc_s[h], l_s[h], lam, sg_ref[...], out_scale)
                o_ref[s_loc, :, cols] = o.astype(o_ref.dtype)

    @pl.when(i == 0)
    def _():
        start_unit(i, 0)

    x1 = (x_ref[...]
          + jnp.dot(a_ref[...], woa_ref[...], preferred_element_type=F32)
          + jnp.dot(py_ref[...], wop_ref[...], preferred_element_type=F32))
    h_act = _rms(x1, g2_ref[...]).astype(BF16)
    y_ref[...] = x1

    ff_done = 0
    for u in range(units):
        if u + 1 < units:
            start_unit(i, u + 1)
        else:
            @pl.when(i + 1 < n_steps)
            def _():
                start_unit(i + 1, 0)
        wait_unit(i, u)
        ff_upto = (u + 1) * n_ff // units
        sc_all = scores(u // n_groups, u % n_groups, u % 2)
        ups = [jnp.dot(h_act, wu_ref[:, c * FF_CHUNK:(c + 1) * FF_CHUNK],
                       preferred_element_type=F32) for c in range(ff_done, ff_upto)]
        absorb(u // n_groups, u % n_groups, u % 2, sc_all)
        for c, uu in zip(range(ff_done, ff_upto), ups):
            uu = jnp.square(jnp.maximum(uu, 0.0)).astype(BF16)
            y_ref[...] += jnp.dot(uu, wd_ref[c * FF_CHUNK:(c + 1) * FF_CHUNK, :],
                                  preferred_element_type=F32)
        ff_done = ff_upto


def _finish_decode(x, a, py, wo_a, wo_p, g2, w_up, w_down,
                   q3, k3, v3, cache_k2, cache_v2, page_table, lq1, lk1, lq2, lk2, sg, layer,
                   page_base):
    n, d = x.shape
    tm = min(TM_FIN, n)
    n_steps = n // tm
    db, n_t, width = q3.shape
    n_pages = page_table.shape[1]
    pps = PAGES_PER_STEP
    d_ff = w_up.shape[1]
    assert n % tm == 0 and db % n_steps == 0 and n_pages % pps == 0
    seqs = db // n_steps
    assert (seqs * (n_pages // pps)) % 2 == 0
    tile = lambda i, pt: (i, 0)
    const = lambda i, pt: (0, 0)
    seq = lambda i, pt: (i, 0, 0)
    once = pl.Buffered(1)
    page_rows = PAGE_SIZE * N_HEADS
    kern = functools.partial(_finish_decode_kernel, _lambda_init(layer), 1.0 - _lambda_init(layer),
                             n_pages, page_base, seqs)
    grid_spec = pltpu.PrefetchScalarGridSpec(
        num_scalar_prefetch=1,
        grid=(n_steps,),
        in_specs=[
            pl.BlockSpec((tm, d), tile),
            pl.BlockSpec((tm, a.shape[1]), tile),
            pl.BlockSpec((tm, py.shape[1]), tile),
            pl.BlockSpec(wo_a.shape, const, pipeline_mode=once),
            pl.BlockSpec(wo_p.shape, const, pipeline_mode=once),
            pl.BlockSpec((1, d), const),
            pl.BlockSpec((d, d_ff), const, pipeline_mode=once),
            pl.BlockSpec((d_ff, d), const, pipeline_mode=once),
            pl.BlockSpec((1, HEAD_DIM), const), pl.BlockSpec((1, HEAD_DIM), const),
            pl.BlockSpec((1, HEAD_DIM), const), pl.BlockSpec((1, HEAD_DIM), const),
            pl.BlockSpec((1, V_DIM), const),
            pl.BlockSpec((seqs, n_t, width), seq),
            pl.BlockSpec((seqs, n_t, width), seq),
            pl.BlockSpec((seqs, n_t, width), seq),
            pl.BlockSpec(memory_space=pl.ANY),
            pl.BlockSpec(memory_space=pl.ANY),
        ],
        out_specs=[
            pl.BlockSpec((tm, d), tile),
            pl.BlockSpec((seqs, n_t, width), seq),
        ],
        scratch_shapes=[
            pltpu.VMEM((2, pps * page_rows, V_DIM), F32),
            pltpu.VMEM((2, pps * page_rows, V_DIM), F32),
            pltpu.SemaphoreType.DMA((2,)),
            pltpu.VMEM((N_HEADS, 2 * n_t, LANES), F32), pltpu.VMEM((N_HEADS, 2 * n_t, LANES), F32),
            pltpu.VMEM((N_HEADS, 2 * n_t, V_DIM), F32),
            pltpu.VMEM((PAGE_SIZE, width), BF16), pltpu.VMEM((PAGE_SIZE, width), BF16),
            pltpu.VMEM((LANES, width), BF16),
        ],
    )
    assert N_HEADS * 2 * n_t <= LANES
    return pl.pallas_call(
        kern,
        grid_spec=grid_spec,
        out_shape=[jax.ShapeDtypeStruct((n, d), F32),
                   jax.ShapeDtypeStruct((db, n_t, width), BF16)],
        compiler_params=pltpu.CompilerParams(
            dimension_semantics=("arbitrary",), vmem_limit_bytes=VMEM_LIMIT),
        name="finish_decode",
    )(page_table.reshape(-1), x, a, py, wo_a, wo_p, g2, w_up, w_down,
      lq1, lk1, lq2, lk2, sg, q3, k3, v3, cache_k2, cache_v2)


def _rope_tables(pos):
    half = HEAD_DIM // 2
    inv = ROPE_THETA ** (-jnp.arange(0, HEAD_DIM, 2, dtype=F32) / HEAD_DIM)
    ang = pos.astype(F32)[:, None] * inv[None, :]
    reps = LANES // half
    sign = jnp.tile(jnp.concatenate([-jnp.ones((half,), F32), jnp.ones((half,), F32)]),
                    LANES // HEAD_DIM)
    return jnp.tile(jnp.cos(ang), (1, reps)), jnp.tile(jnp.sin(ang), (1, reps)) * sign[None, :]


def kernel(x_prompt, x_sample, cache_k, cache_v, state_pool, page_table, ln1_g, w_in, q_norm_g, k_norm_g, lambda_q1, lambda_k1, lambda_q2, lambda_k2, subln_g, pool_map, pool_scale, w_out, ln2_g, w_up, w_down):
    b, s, d = x_prompt.shape
    db, n_t, _ = x_sample.shape
    depth = w_in.shape[0]
    n_pool = cache_k.shape[1]
    pw = pool_scale.shape[1]

    idx = jnp.arange(MXU_DIM) // HEAD_DIM
    gsum = (idx[:, None] == idx[None, :]).astype(BF16)
    cos_p, sin_p = _rope_tables(jnp.arange(s))
    cos_s, sin_s = _rope_tables(PAST_LEN + jnp.arange(n_t))
    row = lambda v: v.reshape(1, -1)

    xp = x_prompt
    xs = x_sample.reshape(db * n_t, d)
    outs = [[] for _ in range(6)]
    for l in range(depth):
        w_in_b = w_in[l].astype(BF16)
        gq = row(jnp.tile(q_norm_g[l], MXU_DIM // HEAD_DIM))
        gk = row(jnp.tile(k_norm_g[l], MXU_DIM // HEAD_DIM))
        pm = pool_map[l].astype(BF16)
        ps = row(pool_scale[l])
        wo = w_out[l].astype(BF16)
        wo_a, wo_p = wo[:ATTN_WIDTH], wo[ATTN_WIDTH:]
        wu = w_up[l].astype(BF16)
        wd = w_down[l].astype(BF16)
        lams = [row(v[l]) for v in (lambda_q1, lambda_k1, lambda_q2, lambda_k2)]
        sg = row(subln_g[l])

        q, k, v, py, tail = _inproj_prompt(xp, row(ln1_g[l]), w_in_b, gq, gk, gsum,
                                           cos_p, sin_p, pm, ps)
        a = _attn_prompt(q, k, v, *lams, sg, l)
        qs, ks, vs, pys, p_new = _inproj_sample(
            xs.reshape(db, n_t * d), row(ln1_g[l]), w_in_b, gq, gk, gsum, cos_s, sin_s, pm, ps,
            state_pool[l].reshape(db, POOL_BUF * pw), n_t, PAST_LEN)

        xp, a_s = _finish_decode(
            xp.reshape(b * s, d), a.reshape(b * s, ATTN_WIDTH), py.reshape(b * s, pw),
            wo_a, wo_p, row(ln2_g[l]), wu, wd,
            qs.reshape(db, n_t, QK_WIDTH), ks.reshape(db, n_t, QK_WIDTH),
            vs.reshape(db, n_t, ATTN_WIDTH),
            cache_k.reshape(-1, V_DIM), cache_v.reshape(-1, V_DIM),
            page_table, *lams, sg, l, l * n_pool)
        xp = xp.reshape(b, s, d)
        outs[0].append(k.reshape(b, s, N_HEADS, V_DIM))
        outs[1].append(v.reshape(b, s, N_HEADS, V_DIM))
        outs[2].append(tail)
        xs = _finish(xs, a_s.reshape(db * n_t, ATTN_WIDTH), pys.reshape(db * n_t, pw),
                     wo_a, wo_p, row(ln2_g[l]), wu, wd)
        outs[3].append(ks.reshape(db, n_t, N_HEADS, V_DIM))
        outs[4].append(vs.reshape(db, n_t, N_HEADS, V_DIM))
        seq_tail = jnp.concatenate([state_pool[l], p_new.reshape(db, n_t, pw)], axis=1)
        outs[5].append(seq_tail[:, -POOL_BUF:])

    return (xp, xs.reshape(db, n_t, d)) + tuple(jnp.stack(o) for o in outs)
```

```python
import functools
import math

import jax
import jax.numpy as jnp
from jax import lax
from jax.experimental import pallas as pl
from jax.experimental.pallas import tpu as pltpu

F32 = jnp.float32
BF16 = jnp.bfloat16

N_HEADS = 4
HEAD_DIM = 64
V_DIM = 2 * HEAD_DIM
QK_WIDTH = N_HEADS * 2 * HEAD_DIM
ATTN_WIDTH = N_HEADS * V_DIM
POOL_WINDOWS = (2, 4, 8, 16)
POOL_BUF = max(POOL_WINDOWS) - 1
ROPE_THETA = 10000.0
EPS = 1e-6
NEG = -1e30
Q_SCALE = HEAD_DIM ** -0.5 * math.log2(math.e)
PAST_LEN = 8192
PAGE_SIZE = 128

LANES = 128
SUBLANES = 8
MXU_DIM = 256
VMEM_LIMIT = 56 * 1024 * 1024

HALO = 2 * SUBLANES
TM_PROJ = 1024
TQ_ATTN = 512
ROW_BLOCK = 256
TM_FIN = 512
FF_CHUNK = 1024
PAGES_PER_STEP = 16


def _lambda_init(layer):
    return 0.8 - 0.6 * math.exp(-0.3 * layer)


def _rms(x, g):
    ms = jnp.mean(x * x, axis=-1, keepdims=True)
    return x * lax.rsqrt(ms + EPS) * g


def _qk_norm_rope(z, gain, gsum, cos, sin):
    ss = jnp.dot((z * z).astype(BF16), gsum, preferred_element_type=F32)
    zn = z * lax.rsqrt(ss * (1.0 / HEAD_DIM) + EPS) * gain
    lane = lax.broadcasted_iota(jnp.int32, (1, LANES), 1)
    first_half = (lane % HEAD_DIM) < (HEAD_DIM // 2)
    outs = []
    for c in range(z.shape[1] // LANES):
        zc = zn[:, c * LANES:(c + 1) * LANES]
        fwd = pltpu.roll(zc, LANES - HEAD_DIM // 2, 1)
        bwd = pltpu.roll(zc, HEAD_DIM // 2, 1)
        outs.append(zc * cos + jnp.where(first_half, fwd, bwd) * sin)
    return outs


def _pool_diffs(hist_rows, p, inv_cnt):
    diffs = []
    for g, w in enumerate(POOL_WINDOWS):
        acc = p[:, g * LANES:(g + 1) * LANES]
        for j in range(1, w):
            acc = acc + hist_rows(j, g)
        diffs.append(acc * inv_cnt[g] - p[:, g * LANES:(g + 1) * LANES])
    return diffs


def _pool_project(diffs, pm_ref, ps_ref):
    outs = []
    for g in range(len(POOL_WINDOWS)):
        y = jnp.dot(diffs[g].astype(BF16), pm_ref[g], preferred_element_type=F32)
        outs.append(y * ps_ref[:, g * LANES:(g + 1) * LANES])
    return outs


def _inproj_prompt_kernel(x_ref, g1_ref, w_ref, gq_ref, gk_ref, gsum_ref, cos_ref, sin_ref,
                          pm_ref, ps_ref,
                          q_ref, k_ref, v_ref, py_ref, tail_ref, pbuf):
    i = pl.program_id(1)
    tm = x_ref.shape[0]
    pw = pbuf.shape[1]

    @pl.when(i == 0)
    def _():
        pbuf[0:HALO, :] = jnp.zeros((HALO, pw), F32)

    @pl.when(i > 0)
    def _():
        pbuf[0:HALO, :] = pbuf[tm:tm + HALO, :]

    xn = _rms(x_ref[...], g1_ref[...]).astype(BF16)
    cos = cos_ref[...]
    sin = sin_ref[...]
    gsum = gsum_ref[...]

    n_blk = QK_WIDTH // MXU_DIM
    zq = [jnp.dot(xn, w_ref[:, blk * MXU_DIM:(blk + 1) * MXU_DIM], preferred_element_type=F32)
          for blk in range(n_blk)]
    zk = [jnp.dot(xn, w_ref[:, QK_WIDTH + blk * MXU_DIM:QK_WIDTH + (blk + 1) * MXU_DIM],
                  preferred_element_type=F32) for blk in range(n_blk)]
    v = jnp.dot(xn, w_ref[:, 2 * QK_WIDTH:2 * QK_WIDTH + ATTN_WIDTH], preferred_element_type=F32)
    p = jnp.dot(xn, w_ref[:, 2 * QK_WIDTH + ATTN_WIDTH:], preferred_element_type=F32)

    for blk in range(n_blk):
        for c, o in enumerate(_qk_norm_rope(zq[blk], gq_ref[...], gsum, cos, sin)):
            col = blk * MXU_DIM + c * LANES
            q_ref[:, col:col + LANES] = (o * Q_SCALE).astype(q_ref.dtype)
    heads_per_blk = MXU_DIM // V_DIM
    for blk in range(n_blk):
        for c, o in enumerate(_qk_norm_rope(zk[blk], gk_ref[...], gsum, cos, sin)):
            k_ref[pl.ds(blk * heads_per_blk + c, tm, stride=N_HEADS), :] = o
    for h in range(N_HEADS):
        v_ref[pl.ds(h, tm, stride=N_HEADS), :] = v[:, h * V_DIM:(h + 1) * V_DIM]

    pbuf[HALO:HALO + tm, :] = p

    pos = i * tm + lax.broadcasted_iota(jnp.int32, (tm, 1), 0)
    inv_cnt = [1.0 / jnp.minimum(pos + 1, w).astype(F32) for w in POOL_WINDOWS]

    diffs = []
    for g, w in enumerate(POOL_WINDOWS):
        cols = slice(g * LANES, (g + 1) * LANES)
        s_w = pbuf[:, cols]
        k = 1
        while k < w:
            s_w = s_w + jnp.concatenate([jnp.zeros((k, LANES), F32), s_w[:-k]], axis=0)
            k *= 2
        diffs.append(s_w[HALO:] * inv_cnt[g] - p[:, cols])
    for g, y in enumerate(_pool_project(diffs, pm_ref, ps_ref)):
        py_ref[:, g * LANES:(g + 1) * LANES] = y.astype(py_ref.dtype)

    @pl.when(i == pl.num_programs(1) - 1)
    def _():
        tail_ref[...] = pbuf[HALO + tm - POOL_BUF:HALO + tm, :]


def _inproj_prompt(x, g1, w_in, gq, gk, gsum, cos, sin, pm, ps):
    b, s, d = x.shape
    tm = TM_PROJ
    assert s % tm == 0
    in_w = w_in.shape[1]
    pw = in_w - 2 * QK_WIDTH - ATTN_WIDTH
    const2 = lambda bi, i: (0, 0)
    tile = lambda bi, i: (bi, i, 0)
    return pl.pallas_call(
        _inproj_prompt_kernel,
        grid=(b, s // tm),
        in_specs=[
            pl.BlockSpec((None, tm, d), tile),
            pl.BlockSpec((1, d), const2),
            pl.BlockSpec((d, in_w), const2),
            pl.BlockSpec((1, MXU_DIM), const2),
            pl.BlockSpec((1, MXU_DIM), const2),
            pl.BlockSpec((MXU_DIM, MXU_DIM), const2),
            pl.BlockSpec((tm, LANES), lambda bi, i: (i, 0)),
            pl.BlockSpec((tm, LANES), lambda bi, i: (i, 0)),
            pl.BlockSpec((len(POOL_WINDOWS), LANES, LANES), lambda bi, i: (0, 0, 0)),
            pl.BlockSpec((1, pw), const2),
        ],
        out_specs=[
            pl.BlockSpec((None, tm, QK_WIDTH), tile),
            pl.BlockSpec((tm * N_HEADS, V_DIM), lambda bi, i: (bi * (s // tm) + i, 0)),
            pl.BlockSpec((tm * N_HEADS, V_DIM), lambda bi, i: (bi * (s // tm) + i, 0)),
            pl.BlockSpec((None, tm, pw), tile),
            pl.BlockSpec((None, POOL_BUF, pw), lambda bi, i: (bi, 0, 0)),
        ],
        out_shape=[
            jax.ShapeDtypeStruct((b, s, QK_WIDTH), BF16),
            jax.ShapeDtypeStruct((b * s * N_HEADS, V_DIM), F32),
            jax.ShapeDtypeStruct((b * s * N_HEADS, V_DIM), F32),
            jax.ShapeDtypeStruct((b, s, pw), BF16),
            jax.ShapeDtypeStruct((b, POOL_BUF, pw), F32),
        ],
        scratch_shapes=[pltpu.VMEM((HALO + tm, pw), F32)],
        compiler_params=pltpu.CompilerParams(
            dimension_semantics=("arbitrary", "arbitrary"), vmem_limit_bytes=VMEM_LIMIT),
        name="inproj_prompt",
    )(x, g1, w_in, gq, gk, gsum, cos, sin, pm, ps)


def _inproj_sample_kernel(past_len, x_ref, g1_ref, w_ref, gq_ref, gk_ref, gsum_ref, cos_ref,
                          sin_ref, pm_ref, ps_ref, state_ref,
                          q_ref, k_ref, v_ref, py_ref, p_ref, hist):
    t = pl.program_id(0)
    pw = p_ref.shape[1]
    xn = _rms(x_ref[...], g1_ref[...]).astype(BF16)
    cos = cos_ref[pl.ds(t, 1), :]
    sin = sin_ref[pl.ds(t, 1), :]
    gsum = gsum_ref[...]

    for blk in range(QK_WIDTH // MXU_DIM):
        zq = jnp.dot(xn, w_ref[:, blk * MXU_DIM:(blk + 1) * MXU_DIM], preferred_element_type=F32)
        for c, o in enumerate(_qk_norm_rope(zq, gq_ref[...], gsum, cos, sin)):
            col = blk * MXU_DIM + c * LANES
            q_ref[:, col:col + LANES] = (o * Q_SCALE).astype(q_ref.dtype)
    for blk in range(QK_WIDTH // MXU_DIM):
        c0 = QK_WIDTH + blk * MXU_DIM
        zk = jnp.dot(xn, w_ref[:, c0:c0 + MXU_DIM], preferred_element_type=F32)
        for c, o in enumerate(_qk_norm_rope(zk, gk_ref[...], gsum, cos, sin)):
            col = blk * MXU_DIM + c * LANES
            k_ref[:, col:col + LANES] = o
    v_ref[...] = jnp.dot(xn, w_ref[:, 2 * QK_WIDTH:2 * QK_WIDTH + ATTN_WIDTH],
                         preferred_element_type=F32)
    p = jnp.dot(xn, w_ref[:, 2 * QK_WIDTH + ATTN_WIDTH:], preferred_element_type=F32)
    p_ref[...] = p

    db = p.shape[0]

    @pl.when(t == 0)
    def _():
        for s in range(POOL_BUF):
            hist[s * db:(s + 1) * db, :] = state_ref[:, s * pw:(s + 1) * pw]

    hist[pl.ds(pl.multiple_of((POOL_BUF + t) * db, SUBLANES), db), :] = p

    pos1 = jnp.full((1, LANES), past_len + 1, jnp.int32) + t
    inv_cnt = [1.0 / jnp.minimum(pos1, w).astype(F32) for w in POOL_WINDOWS]

    def hist_rows(j, g):
        start = pl.multiple_of((POOL_BUF + t - j) * db, SUBLANES)
        return hist[pl.ds(start, db), g * LANES:(g + 1) * LANES]

    diffs = _pool_diffs(hist_rows, p, inv_cnt)
    for g, y in enumerate(_pool_project(diffs, pm_ref, ps_ref)):
        py_ref[:, g * LANES:(g + 1) * LANES] = y.astype(py_ref.dtype)


def _inproj_sample(x2, g1, w_in, gq, gk, gsum, cos, sin, pm, ps, state2, n_t, past_len):
    db = x2.shape[0]
    d = x2.shape[1] // n_t
    in_w = w_in.shape[1]
    pw = in_w - 2 * QK_WIDTH - ATTN_WIDTH
    const2 = lambda t: (0, 0)
    step = lambda t: (0, t)
    return pl.pallas_call(
        functools.partial(_inproj_sample_kernel, past_len),
        grid=(n_t,),
        in_specs=[
            pl.BlockSpec((db, d), step),
            pl.BlockSpec((1, d), const2),
            pl.BlockSpec((d, in_w), const2),
            pl.BlockSpec((1, MXU_DIM), const2),
            pl.BlockSpec((1, MXU_DIM), const2),
            pl.BlockSpec((MXU_DIM, MXU_DIM), const2),
            pl.BlockSpec((n_t, LANES), const2),
            pl.BlockSpec((n_t, LANES), const2),
            pl.BlockSpec((len(POOL_WINDOWS), LANES, LANES), lambda t: (0, 0, 0)),
            pl.BlockSpec((1, pw), const2),
            pl.BlockSpec((db, POOL_BUF * pw), const2),
        ],
        out_specs=[
            pl.BlockSpec((db, QK_WIDTH), step),
            pl.BlockSpec((db, QK_WIDTH), step),
            pl.BlockSpec((db, ATTN_WIDTH), step),
            pl.BlockSpec((db, pw), step),
            pl.BlockSpec((db, pw), step),
        ],
        out_shape=[
            jax.ShapeDtypeStruct((db, n_t * QK_WIDTH), BF16),
            jax.ShapeDtypeStruct((db, n_t * QK_WIDTH), F32),
            jax.ShapeDtypeStruct((db, n_t * ATTN_WIDTH), F32),
            jax.ShapeDtypeStruct((db, n_t * pw), BF16),
            jax.ShapeDtypeStruct((db, n_t * pw), F32),
        ],
        scratch_shapes=[pltpu.VMEM(((POOL_BUF + n_t) * db, pw), F32)],
        compiler_params=pltpu.CompilerParams(
            dimension_semantics=("arbitrary",), vmem_limit_bytes=VMEM_LIMIT),
        name="inproj_sample",
    )(x2, g1, w_in, gq, gk, gsum, cos, sin, pm, ps, state2)


def _diff_lambda(lq1_ref, lk1_ref, lq2_ref, lk2_ref, lam0):
    a = jnp.sum(lq1_ref[...] * lk1_ref[...], axis=-1, keepdims=True)
    b = jnp.sum(lq2_ref[...] * lk2_ref[...], axis=-1, keepdims=True)
    return jnp.exp(a) - jnp.exp(b) + lam0


def _split_maps(q):
    lane = lax.broadcasted_iota(jnp.int32, (1, LANES), 1)
    zero = jnp.zeros_like(q)
    return jnp.concatenate([jnp.where(lane < HEAD_DIM, q, zero),
                            jnp.where(lane >= HEAD_DIM, q, zero)], axis=0)


def _subln_out(acc, l, lam, sg, scale):
    m = acc.shape[0] // 2
    o = acc[:m] / l[:m] - lam * (acc[m:] / l[m:])
    return _rms(o, sg) * scale


def _attn_prompt_kernel(lam0, out_scale, lq1_ref, lk1_ref, lq2_ref, lk2_ref, sg_ref,
                        q_ref, k_ref, v_ref, o_ref, kb, vb1, q2, m_s, acc_s):
    s = q_ref.shape[0]
    tq = TQ_ATTN
    rb = ROW_BLOCK
    nq = s // tq
    lam = _diff_lambda(lq1_ref, lk1_ref, lq2_ref, lk2_ref, lam0)
    for h in range(N_HEADS):
        kb[h] = k_ref[pl.ds(h, s, stride=N_HEADS), :].astype(BF16)
        vb1[h, :, 0:V_DIM] = v_ref[pl.ds(h, s, stride=N_HEADS), :].astype(BF16)
        vb1[h, :, V_DIM:] = jnp.ones((s, V_DIM), BF16)

    def chunk(kstart, diagonal):
        for h in range(N_HEADS):
            for r in range(2 * tq // rb):
                rows = slice(r * rb, (r + 1) * rb)
                p0 = (r * rb) % tq
                nk = p0 + rb if diagonal else tq
                sc = lax.dot_general(q2[h, rows, :], kb[h, kstart:kstart + nk, :],
                                     (((1,), (1,)), ((), ())), preferred_element_type=F32)
                if diagonal:
                    row = lax.broadcasted_iota(jnp.int32, (rb, rb), 0)
                    col = lax.broadcasted_iota(jnp.int32, (rb, rb), 1)
                    tri = jnp.where(col <= row, sc[:, p0:], NEG)
                    sc = tri if p0 == 0 else jnp.concatenate([sc[:, :p0], tri], axis=1)
                    m_new = jnp.broadcast_to(jnp.max(sc, axis=-1, keepdims=True), (rb, LANES))
                else:
                    m_prev = m_s[h, rows, :]
                    m_new = jnp.maximum(m_prev, jnp.max(sc, axis=-1, keepdims=True))
                p = jnp.exp2(sc - jnp.tile(m_new, (1, nk // LANES))).astype(BF16)
                pv = jnp.dot(p, vb1[h, kstart:kstart + nk, :], preferred_element_type=F32)
                if diagonal:
                    acc_s[h, rows, :] = pv
                else:
                    alpha = jnp.exp2(m_prev - m_new)
                    acc_s[h, rows, :] = jnp.tile(alpha, (1, 2)) * acc_s[h, rows, :] + pv
                m_s[h, rows, :] = m_new

    for qi in range(nq):
        qstart = qi * tq
        for h in range(N_HEADS):
            q2[h] = _split_maps(q_ref[qstart:qstart + tq, h * V_DIM:(h + 1) * V_DIM])
        chunk(qstart, True)
        for kj in range(qi):
            chunk(kj * tq, False)
        for h in range(N_HEADS):
            acc = acc_s[h]
            o = _subln_out(acc[:, 0:V_DIM], acc[:, V_DIM:], lam, sg_ref[...], out_scale)
            o_ref[qstart:qstart + tq, h * V_DIM:(h + 1) * V_DIM] = o.astype(o_ref.dtype)


def _attn_prompt(q, k2, v2, lq1, lk1, lq2, lk2, sg, layer):
    b, s, _ = q.shape
    tq = TQ_ATTN
    assert s % tq == 0 and tq % ROW_BLOCK == 0
    vec = lambda bi: (0, 0)
    kern = functools.partial(_attn_prompt_kernel, _lambda_init(layer), 1.0 - _lambda_init(layer))
    return pl.pallas_call(
        kern,
        grid=(b,),
        in_specs=[
            pl.BlockSpec((1, HEAD_DIM), vec), pl.BlockSpec((1, HEAD_DIM), vec),
            pl.BlockSpec((1, HEAD_DIM), vec), pl.BlockSpec((1, HEAD_DIM), vec),
            pl.BlockSpec((1, V_DIM), vec),
            pl.BlockSpec((None, s, ATTN_WIDTH), lambda bi: (bi, 0, 0)),
            pl.BlockSpec((s * N_HEADS, V_DIM), lambda bi: (bi, 0)),
            pl.BlockSpec((s * N_HEADS, V_DIM), lambda bi: (bi, 0)),
        ],
        out_specs=pl.BlockSpec((None, s, ATTN_WIDTH), lambda bi: (bi, 0, 0)),
        out_shape=jax.ShapeDtypeStruct((b, s, ATTN_WIDTH), BF16),
        scratch_shapes=[
            pltpu.VMEM((N_HEADS, s, V_DIM), BF16), pltpu.VMEM((N_HEADS, s, 2 * V_DIM), BF16),
            pltpu.VMEM((N_HEADS, 2 * tq, V_DIM), BF16),
            pltpu.VMEM((N_HEADS, 2 * tq, LANES), F32),
            pltpu.VMEM((N_HEADS, 2 * tq, 2 * V_DIM), F32),
        ],
        compiler_params=pltpu.CompilerParams(
            dimension_semantics=("arbitrary",), vmem_limit_bytes=VMEM_LIMIT),
        name="attn_prompt",
    )(lq1, lk1, lq2, lk2, sg, q, k2, v2)


def _finish_kernel(x_ref, a_ref, py_ref, woa_ref, wop_ref, g2_ref, wu_ref, wd_ref, y_ref):
    x1 = (x_ref[...]
          + jnp.dot(a_ref[...], woa_ref[...], preferred_element_type=F32)
          + jnp.dot(py_ref[...], wop_ref[...], preferred_element_type=F32))
    h = _rms(x1, g2_ref[...]).astype(BF16)
    y_ref[...] = x1
    for c in range(wu_ref.shape[1] // FF_CHUNK):
        u = jnp.dot(h, wu_ref[:, c * FF_CHUNK:(c + 1) * FF_CHUNK], preferred_element_type=F32)
        u = jnp.square(jnp.maximum(u, 0.0)).astype(BF16)
        y_ref[...] += jnp.dot(u, wd_ref[c * FF_CHUNK:(c + 1) * FF_CHUNK, :],
                              preferred_element_type=F32)


def _finish(x, a, py, wo_a, wo_p, g2, w_up, w_down):
    n, d = x.shape
    tm = min(TM_FIN, n)
    assert n % tm == 0
    d_ff = w_up.shape[1]
    tile = lambda i: (i, 0)
    const = lambda i: (0, 0)
    once = pl.Buffered(1)
    return pl.pallas_call(
        _finish_kernel,
        grid=(n // tm,),
        in_specs=[
            pl.BlockSpec((tm, d), tile),
            pl.BlockSpec((tm, a.shape[1]), tile),
            pl.BlockSpec((tm, py.shape[1]), tile),
            pl.BlockSpec(wo_a.shape, const, pipeline_mode=once),
            pl.BlockSpec(wo_p.shape, const, pipeline_mode=once),
            pl.BlockSpec((1, d), const),
            pl.BlockSpec((d, d_ff), const, pipeline_mode=once),
            pl.BlockSpec((d_ff, d), const, pipeline_mode=once),
        ],
        out_specs=pl.BlockSpec((tm, d), tile),
        out_shape=jax.ShapeDtypeStruct((n, d), F32),
        compiler_params=pltpu.CompilerParams(
            dimension_semantics=("arbitrary",), vmem_limit_bytes=VMEM_LIMIT),
        name="finish",
    )(x, a, py, wo_a, wo_p, g2, w_up, w_down)


def _finish_decode_kernel(lam0, out_scale, n_pages, page_base, seqs,
                          pt_ref,
                          x_ref, a_ref, py_ref, woa_ref, wop_ref, g2_ref, wu_ref, wd_ref,
                          lq1_ref, lk1_ref, lq2_ref, lk2_ref, sg_ref, q_ref, kn_ref, vn_ref,
                          ck_ref, cv_ref,
                          y_ref, o_ref,
                          kbuf, vbuf, sem, m_s, l_s, acc_s, knp, vnp, qbd):
    i = pl.program_id(0)
    n_steps = pl.num_programs(0)
    pps = PAGES_PER_STEP
    n_groups = n_pages // pps
    units = seqs * n_groups
    page_rows = PAGE_SIZE * N_HEADS
    n_t = q_ref.shape[1]
    n_ff = wu_ref.shape[1] // FF_CHUNK

    def page_copies(seq, g, slot):
        out = []
        for j in range(pps):
            page = page_base + pt_ref[seq * n_pages + g * pps + j]
            src = pl.ds(pl.multiple_of(page * page_rows, page_rows), page_rows)
            dst = pl.ds(j * page_rows, page_rows)
            out.append(pltpu.make_async_copy(ck_ref.at[src, :], kbuf.at[slot, dst, :], sem.at[slot]))
            out.append(pltpu.make_async_copy(cv_ref.at[src, :], vbuf.at[slot, dst, :], sem.at[slot]))
        return out

    def start_unit(step, u):
        for cp in page_copies(step * seqs + u // n_groups, u % n_groups, u % 2):
            cp.start()

    def wait_unit(step, u):
        for cp in page_copies(step * seqs + u // n_groups, u % n_groups, u % 2):
            cp.wait()

    def update(h, sc, vv):
        m_prev = m_s[h]
        m_new = jnp.maximum(m_prev, jnp.max(sc, axis=-1, keepdims=True))
        alpha = jnp.exp2(m_prev - m_new)
        p = jnp.exp2(sc - m_new[:, 0:1])
        l_s[h] = alpha * l_s[h] + jnp.sum(p, axis=-1, keepdims=True)
        acc_s[h] = alpha * acc_s[h] + jnp.dot(p.astype(BF16), vv, preferred_element_type=F32)
        m_s[h] = m_new

    def head_rows(buf, slot, h):
        return jnp.concatenate(
            [buf[slot, pl.ds(j * page_rows + h, PAGE_SIZE, stride=N_HEADS), :].astype(BF16)
             for j in range(pps)], axis=0)

    def load_keys(s_loc, g, slot):
        if g == 0:
            m_s[...] = jnp.full(m_s.shape, NEG, F32)
            l_s[...] = jnp.zeros(l_s.shape, F32)
            acc_s[...] = jnp.zeros(acc_s.shape, F32)
            qbd[...] = jnp.zeros(qbd.shape, qbd.dtype)
            for h in range(N_HEADS):
                cols = slice(h * V_DIM, (h + 1) * V_DIM)
                qbd[h * 2 * n_t:(h + 1) * 2 * n_t, cols] = _split_maps(q_ref[s_loc, :, cols])
        return jnp.concatenate([head_rows(kbuf, slot, h) for h in range(N_HEADS)], axis=1)

    def scores(k_all):
        sc_t = lax.dot_general(k_all, qbd[...], (((1,), (1,)), ((), ())),
                               preferred_element_type=F32)
        return sc_t.T

    def absorb(s_loc, g, slot, sc_all):
        for h in range(N_HEADS):
            update(h, sc_all[h * 2 * n_t:(h + 1) * 2 * n_t, :], head_rows(vbuf, slot, h))
        if g == n_groups - 1:
            lam = _diff_lambda(lq1_ref, lk1_ref, lq2_ref, lk2_ref, lam0)
            knp[...] = jnp.zeros(knp.shape, knp.dtype)
            vnp[...] = jnp.zeros(vnp.shape, vnp.dtype)
            knp[0:n_t, :] = kn_ref[s_loc].astype(BF16)
            vnp[0:n_t, :] = vn_ref[s_loc].astype(BF16)
            row = lax.broadcasted_iota(jnp.int32, (2 * n_t, PAGE_SIZE), 0) % n_t
            col = lax.broadcasted_iota(jnp.int32, (2 * n_t, PAGE_SIZE), 1)
            for h in range(N_HEADS):
                cols = slice(h * V_DIM, (h + 1) * V_DIM)
                sc = lax.dot_general(_split_maps(q_ref[s_loc, :, cols]), knp[:, cols],
                                     (((1,), (1,)), ((), ())), preferred_element_type=F32)
                update(h, jnp.where(col <= row, sc, NEG), vnp[:, cols])
                o = _subln_out(acc_s[h], l_s[h], lam, sg_ref[...], out_scale)
                o_ref[s_loc, :, cols] = o.astype(o_ref.dtype)

    @pl.when(i == 0)
    def _():
        start_unit(i, 0)

    def prefetch_next(u):
        if u + 1 < units:
            start_unit(i, u + 1)
        else:
            @pl.when(i + 1 < n_steps)
            def _():
                start_unit(i + 1, 0)

    prefetch_next(0)
    wait_unit(i, 0)
    k_all = load_keys(0, 0, 0)

    x1 = (x_ref[...]
          + jnp.dot(a_ref[...], woa_ref[...], preferred_element_type=F32)
          + jnp.dot(py_ref[...], wop_ref[...], preferred_element_type=F32))
    h_act = _rms(x1, g2_ref[...]).astype(BF16)
    y_ref[...] = x1

    ff_done = 0
    for u in range(units):
        if u > 0:
            prefetch_next(u)
            wait_unit(i, u)
            k_all = load_keys(u // n_groups, u % n_groups, u % 2)
        ff_upto = (u + 1) * n_ff // units
        sc_all = scores(k_all)
        ups = [jnp.dot(h_act, wu_ref[:, c * FF_CHUNK:(c + 1) * FF_CHUNK],
                       preferred_element_type=F32) for c in range(ff_done, ff_upto)]
        absorb(u // n_groups, u % n_groups, u % 2, sc_all)
        for c, uu in zip(range(ff_done, ff_upto), ups):
            uu = jnp.square(jnp.maximum(uu, 0.0)).astype(BF16)
            y_ref[...] += jnp.dot(uu, wd_ref[c * FF_CHUNK:(c + 1) * FF_CHUNK, :],
                                  preferred_element_type=F32)
        ff_done = ff_upto


def _finish_decode(x, a, py, wo_a, wo_p, g2, w_up, w_down,
                   q3, k3, v3, cache_k2, cache_v2, page_table, lq1, lk1, lq2, lk2, sg, layer,
                   page_base):
    n, d = x.shape
    tm = min(TM_FIN, n)
    n_steps = n // tm
    db, n_t, width = q3.shape
    n_pages = page_table.shape[1]
    pps = PAGES_PER_STEP
    d_ff = w_up.shape[1]
    assert n % tm == 0 and db % n_steps == 0 and n_pages % pps == 0
    seqs = db // n_steps
    assert (seqs * (n_pages // pps)) % 2 == 0
    tile = lambda i, pt: (i, 0)
    const = lambda i, pt: (0, 0)
    seq = lambda i, pt: (i, 0, 0)
    once = pl.Buffered(1)
    page_rows = PAGE_SIZE * N_HEADS
    kern = functools.partial(_finish_decode_kernel, _lambda_init(layer), 1.0 - _lambda_init(layer),
                             n_pages, page_base, seqs)
    grid_spec = pltpu.PrefetchScalarGridSpec(
        num_scalar_prefetch=1,
        grid=(n_steps,),
        in_specs=[
            pl.BlockSpec((tm, d), tile),
            pl.BlockSpec((tm, a.shape[1]), tile),
            pl.BlockSpec((tm, py.shape[1]), tile),
            pl.BlockSpec(wo_a.shape, const, pipeline_mode=once),
            pl.BlockSpec(wo_p.shape, const, pipeline_mode=once),
            pl.BlockSpec((1, d), const),
            pl.BlockSpec((d, d_ff), const, pipeline_mode=once),
            pl.BlockSpec((d_ff, d), const, pipeline_mode=once),
            pl.BlockSpec((1, HEAD_DIM), const), pl.BlockSpec((1, HEAD_DIM), const),
            pl.BlockSpec((1, HEAD_DIM), const), pl.BlockSpec((1, HEAD_DIM), const),
            pl.BlockSpec((1, V_DIM), const),
            pl.BlockSpec((seqs, n_t, width), seq),
            pl.BlockSpec((seqs, n_t, width), seq),
            pl.BlockSpec((seqs, n_t, width), seq),
            pl.BlockSpec(memory_space=pl.ANY),
            pl.BlockSpec(memory_space=pl.ANY),
        ],
        out_specs=[
            pl.BlockSpec((tm, d), tile),
            pl.BlockSpec((seqs, n_t, width), seq),
        ],
        scratch_shapes=[
            pltpu.VMEM((2, pps * page_rows, V_DIM), F32),
            pltpu.VMEM((2, pps * page_rows, V_DIM), F32),
            pltpu.SemaphoreType.DMA((2,)),
            pltpu.VMEM((N_HEADS, 2 * n_t, LANES), F32), pltpu.VMEM((N_HEADS, 2 * n_t, LANES), F32),
            pltpu.VMEM((N_HEADS, 2 * n_t, V_DIM), F32),
            pltpu.VMEM((PAGE_SIZE, width), BF16), pltpu.VMEM((PAGE_SIZE, width), BF16),
            pltpu.VMEM((LANES, width), BF16),
        ],
    )
    assert N_HEADS * 2 * n_t <= LANES
    return pl.pallas_call(
        kern,
        grid_spec=grid_spec,
        out_shape=[jax.ShapeDtypeStruct((n, d), F32),
                   jax.ShapeDtypeStruct((db, n_t, width), BF16)],
        compiler_params=pltpu.CompilerParams(
            dimension_semantics=("arbitrary",), vmem_limit_bytes=VMEM_LIMIT),
        name="finish_decode",
    )(page_table.reshape(-1), x, a, py, wo_a, wo_p, g2, w_up, w_down,
      lq1, lk1, lq2, lk2, sg, q3, k3, v3, cache_k2, cache_v2)


def _rope_tables(pos):
    half = HEAD_DIM // 2
    inv = ROPE_THETA ** (-jnp.arange(0, HEAD_DIM, 2, dtype=F32) / HEAD_DIM)
    ang = pos.astype(F32)[:, None] * inv[None, :]
    reps = LANES // half
    sign = jnp.tile(jnp.concatenate([-jnp.ones((half,), F32), jnp.ones((half,), F32)]),
                    LANES // HEAD_DIM)
    return jnp.tile(jnp.cos(ang), (1, reps)), jnp.tile(jnp.sin(ang), (1, reps)) * sign[None, :]


def kernel(x_prompt, x_sample, cache_k, cache_v, state_pool, page_table, ln1_g, w_in, q_norm_g, k_norm_g, lambda_q1, lambda_k1, lambda_q2, lambda_k2, subln_g, pool_map, pool_scale, w_out, ln2_g, w_up, w_down):
    b, s, d = x_prompt.shape
    db, n_t, _ = x_sample.shape
    depth = w_in.shape[0]
    n_pool = cache_k.shape[1]
    pw = pool_scale.shape[1]

    idx = jnp.arange(MXU_DIM) // HEAD_DIM
    gsum = (idx[:, None] == idx[None, :]).astype(BF16)
    cos_p, sin_p = _rope_tables(jnp.arange(s))
    cos_s, sin_s = _rope_tables(PAST_LEN + jnp.arange(n_t))
    row = lambda v: v.reshape(1, -1)

    xp = x_prompt
    xs = x_sample.reshape(db * n_t, d)
    outs = [[] for _ in range(6)]
    for l in range(depth):
        w_in_b = w_in[l].astype(BF16)
        gq = row(jnp.tile(q_norm_g[l], MXU_DIM // HEAD_DIM))
        gk = row(jnp.tile(k_norm_g[l], MXU_DIM // HEAD_DIM))
        pm = pool_map[l].astype(BF16)
        ps = row(pool_scale[l])
        wo = w_out[l].astype(BF16)
        wo_a, wo_p = wo[:ATTN_WIDTH], wo[ATTN_WIDTH:]
        wu = w_up[l].astype(BF16)
        wd = w_down[l].astype(BF16)
        lams = [row(v[l]) for v in (lambda_q1, lambda_k1, lambda_q2, lambda_k2)]
        sg = row(subln_g[l])

        q, k, v, py, tail = _inproj_prompt(xp, row(ln1_g[l]), w_in_b, gq, gk, gsum,
                                           cos_p, sin_p, pm, ps)
        a = _attn_prompt(q, k, v, *lams, sg, l)
        qs, ks, vs, pys, p_new = _inproj_sample(
            xs.reshape(db, n_t * d), row(ln1_g[l]), w_in_b, gq, gk, gsum, cos_s, sin_s, pm, ps,
            state_pool[l].reshape(db, POOL_BUF * pw), n_t, PAST_LEN)

        xp, a_s = _finish_decode(
            xp.reshape(b * s, d), a.reshape(b * s, ATTN_WIDTH), py.reshape(b * s, pw),
            wo_a, wo_p, row(ln2_g[l]), wu, wd,
            qs.reshape(db, n_t, QK_WIDTH), ks.reshape(db, n_t, QK_WIDTH),
            vs.reshape(db, n_t, ATTN_WIDTH),
            cache_k.reshape(-1, V_DIM), cache_v.reshape(-1, V_DIM),
            page_table, *lams, sg, l, l * n_pool)
        xp = xp.reshape(b, s, d)
        outs[0].append(k.reshape(b, s, N_HEADS, V_DIM))
        outs[1].append(v.reshape(b, s, N_HEADS, V_DIM))
        outs[2].append(tail)
        xs = _finish(xs, a_s.reshape(db * n_t, ATTN_WIDTH), pys.reshape(db * n_t, pw),
                     wo_a, wo_p, row(ln2_g[l]), wu, wd)
        outs[3].append(ks.reshape(db, n_t, N_HEADS, V_DIM))
        outs[4].append(vs.reshape(db, n_t, N_HEADS, V_DIM))
        seq_tail = jnp.concatenate([state_pool[l], p_new.reshape(db, n_t, pw)], axis=1)
        outs[5].append(seq_tail[:, -POOL_BUF:])

    return (xp, xs.reshape(db, n_t, d)) + tuple(jnp.stack(o) for o in outs)
```
